```python
import math
import jax, jax.numpy as jnp
from jax import lax
import numpy as np

D_MODEL = 1024
BATCH = 16
SEQ = 2048
DEPTH = 2

N_A = max(1, DEPTH // 2)
N_B = DEPTH - N_A
N_HEADS = 16
HEAD_DIM = D_MODEL // N_HEADS
D_FF = 2816
CONV_WIDTH = 3
Q_BLOCK = 128
RMS_EPS = 1e-6

kernel_name = "yoco_shortconv_fox_macaron_sandwich"


def rms_norm(x, g):
    xf = x.astype(jnp.float32)
    y = xf * lax.rsqrt(jnp.mean(xf * xf, axis=-1, keepdims=True) + RMS_EPS)
    return (y * g.astype(jnp.float32)).astype(x.dtype)


def swiglu(x, w_in, w_out):
    gate, up = jnp.split(x @ w_in, 2, axis=-1)
    return (jax.nn.silu(gate) * up) @ w_out


def short_conv_mixer(x, w_in, conv_k, w_out):
    b_gate, c_gate, h = jnp.split(x @ w_in, 3, axis=-1)
    u = c_gate * h
    y = lax.conv_general_dilated(
        u, conv_k[:, None, :].astype(u.dtype),
        window_strides=(1,), padding=[(CONV_WIDTH - 1, 0)],
        dimension_numbers=("NWC", "WIO", "NWC"),
        feature_group_count=D_MODEL)
    return (b_gate * y) @ w_out


def shared_kv(x, kv_g, kv_w, forget_b):
    bsz, seq, _ = x.shape
    p = rms_norm(x, kv_g) @ kv_w
    k = p[..., :D_MODEL].reshape(bsz, seq, N_HEADS, HEAD_DIM).transpose(0, 2, 1, 3)
    v = p[..., D_MODEL:2 * D_MODEL].reshape(bsz, seq, N_HEADS, HEAD_DIM).transpose(0, 2, 1, 3)
    f_logit = (p[..., 2 * D_MODEL:] + forget_b).astype(jnp.float32)
    log_f = jax.nn.log_sigmoid(f_logit)
    c = jnp.cumsum(log_f, axis=1).transpose(0, 2, 1)
    return k, v, c


def forgetting_attention(x, k, v, c, w_qg, w_o):
    bsz, seq, _ = x.shape
    n_blk = seq // Q_BLOCK
    q, gate = jnp.split(x @ w_qg, 2, axis=-1)
    q = q.reshape(bsz, seq, N_HEADS, HEAD_DIM).transpose(0, 2, 1, 3)
    q_blocks = q.reshape(bsz, N_HEADS, n_blk, Q_BLOCK, HEAD_DIM).transpose(2, 0, 1, 3, 4)
    c_blocks = c.reshape(bsz, N_HEADS, n_blk, Q_BLOCK).transpose(2, 0, 1, 3)
    k_pos = jnp.arange(seq)
    scale = 1.0 / math.sqrt(HEAD_DIM)

    def attend_block(args):
        qb, cb, i = args
        s = jnp.einsum("bhqd,bhkd->bhqk", qb, k, preferred_element_type=jnp.float32) * scale
        s = s + cb[..., None] - c[:, :, None, :]
        q_pos = i * Q_BLOCK + jnp.arange(Q_BLOCK)
        s = jnp.where(k_pos[None, :] <= q_pos[:, None], s, -jnp.inf)
        p = jax.nn.softmax(s, axis=-1)
        return jnp.einsum("bhqk,bhkd->bhqd", p.astype(v.dtype), v)

    o = lax.map(attend_block, (q_blocks, c_blocks, jnp.arange(n_blk)))
    o = o.transpose(1, 0, 3, 2, 4).reshape(bsz, seq, D_MODEL)
    return (jax.nn.sigmoid(gate) * o) @ w_o


def setup_inputs(seed: int = 0) -> dict:
    key = jax.random.key(seed)
    ks = jax.random.split(key, 24)
    f32 = jnp.float32

    def w(k, shape, fan_in):
        return jax.random.normal(k, shape, f32) * fan_in ** -0.5

    def gain(k, shape):
        return 1.0 + 0.05 * jax.random.normal(k, shape, f32)

    return {
        "x": jax.random.normal(ks[0], (BATCH, SEQ, D_MODEL), f32),
        "ffn1_pre_g": gain(ks[1], (DEPTH, D_MODEL)),
        "ffn1_post_g": gain(ks[2], (DEPTH, D_MODEL)),
        "ffn1_w_in": w(ks[3], (DEPTH, D_MODEL, 2 * D_FF), D_MODEL),
        "ffn1_w_out": w(ks[4], (DEPTH, D_FF, D_MODEL), D_FF),
        "mix_pre_g": gain(ks[5], (DEPTH, D_MODEL)),
        "mix_post_g": gain(ks[6], (DEPTH, D_MODEL)),
        "ffn2_pre_g": gain(ks[7], (DEPTH, D_MODEL)),
        "ffn2_post_g": gain(ks[8], (DEPTH, D_MODEL)),
        "ffn2_w_in": w(ks[9], (DEPTH, D_MODEL, 2 * D_FF), D_MODEL),
        "ffn2_w_out": w(ks[10], (DEPTH, D_FF, D_MODEL), D_FF),
        "conv_w_in": w(ks[11], (N_A, D_MODEL, 3 * D_MODEL), D_MODEL),
        "conv_k": w(ks[12], (N_A, CONV_WIDTH, D_MODEL), CONV_WIDTH),
        "conv_w_out": w(ks[13], (N_A, D_MODEL, D_MODEL), D_MODEL),
        "kv_g": gain(ks[14], (D_MODEL,)),
        "kv_w": w(ks[15], (D_MODEL, 2 * D_MODEL + N_HEADS), D_MODEL),
        "forget_b": jax.random.uniform(ks[16], (N_HEADS,), f32, 1.0, 3.0),
        "attn_w_qg": w(ks[17], (N_B, D_MODEL, 2 * D_MODEL), D_MODEL),
        "attn_w_o": w(ks[18], (N_B, D_MODEL, D_MODEL), D_MODEL),
    }


def reference(x, ffn1_pre_g, ffn1_post_g, ffn1_w_in, ffn1_w_out, mix_pre_g, mix_post_g,
              ffn2_pre_g, ffn2_post_g, ffn2_w_in, ffn2_w_out, conv_w_in, conv_k, conv_w_out,
              kv_g, kv_w, forget_b, attn_w_qg, attn_w_o):
    k = v = c = None
    for l in range(DEPTH):
        if l == N_A:
            k, v, c = shared_kv(x, kv_g, kv_w, forget_b)
        h = swiglu(rms_norm(x, ffn1_pre_g[l]), ffn1_w_in[l], ffn1_w_out[l])
        x = x + 0.5 * rms_norm(h, ffn1_post_g[l])
        xn = rms_norm(x, mix_pre_g[l])
        if l < N_A:
            m = short_conv_mixer(xn, conv_w_in[l], conv_k[l], conv_w_out[l])
        else:
            j = l - N_A
            m = forgetting_attention(xn, k, v, c, attn_w_qg[j], attn_w_o[j])
        x = x + rms_norm(m, mix_post_g[l])
        h = swiglu(rms_norm(x, ffn2_pre_g[l]), ffn2_w_in[l], ffn2_w_out[l])
        x = x + 0.5 * rms_norm(h, ffn2_post_g[l])
    return x
```

```python
import functools
import math

import jax
import jax.numpy as jnp
from jax import lax
from jax.experimental import pallas as pl
from jax.experimental.pallas import tpu as pltpu

D_MODEL = 1024
N_HEADS = 16
HEAD_DIM = D_MODEL // N_HEADS
N_PAIRS = N_HEADS // 2
D_FF = 2816
RMS_EPS = 1e-6
LANES = 128
VMEM_LIMIT = 56 * 1024 * 1024

FFN_TM = 512
FFN_FC = 256
MIX_TS = 512
MIX_CC = 256
KV_TS = 512
ATT_T = 256


def _rms(x, g):
    return x * lax.rsqrt(jnp.mean(x * x, axis=-1, keepdims=True) + RMS_EPS) * g


def _dot(a, b):
    return jnp.dot(a, b, preferred_element_type=jnp.float32)


def _resident(shape):
    nd = len(shape)
    return pl.BlockSpec(shape, lambda *_: (0,) * nd, pipeline_mode=pl.Buffered(1))


def _ffn_kernel(x_ref, pre_ref, post_ref, win_ref, wout_ref, o_ref):
    x = x_ref[...]
    xn = _rms(x, pre_ref[...]).astype(jnp.bfloat16)
    acc = jnp.zeros(x.shape, jnp.float32)
    for c in range(D_FF // FFN_FC):
        lo = c * FFN_FC
        gate = _dot(xn, win_ref[:, lo:lo + FFN_FC])
        up = _dot(xn, win_ref[:, D_FF + lo:D_FF + lo + FFN_FC])
        h = (gate * jax.nn.sigmoid(gate) * up).astype(jnp.bfloat16)
        acc = acc + _dot(h, wout_ref[lo:lo + FFN_FC, :])
    o_ref[...] = x + 0.5 * _rms(acc, post_ref[...])


def _ffn(x2d, pre_g, post_g, w_in, w_out):
    t = x2d.shape[0]
    return pl.pallas_call(
        _ffn_kernel,
        name="ffn",
        grid=(t // FFN_TM,),
        in_specs=[
            pl.BlockSpec((FFN_TM, D_MODEL), lambda i: (i, 0)),
            _resident((1, D_MODEL)),
            _resident((1, D_MODEL)),
            _resident((D_MODEL, 2 * D_FF)),
            _resident((D_FF, D_MODEL)),
        ],
        out_specs=pl.BlockSpec((FFN_TM, D_MODEL), lambda i: (i, 0)),
        out_shape=jax.ShapeDtypeStruct(x2d.shape, jnp.float32),
        compiler_params=pltpu.CompilerParams(
            dimension_semantics=("arbitrary",), vmem_limit_bytes=VMEM_LIMIT),
    )(x2d, pre_g.reshape(1, -1), post_g.reshape(1, -1), w_in, w_out)


def _mix_kernel(x_ref, pre_ref, post_ref, win_ref, ck_ref, wout_ref, o_ref, carry_ref):
    @pl.when(pl.program_id(1) == 0)
    def _():
        carry_ref[...] = jnp.zeros_like(carry_ref)

    x = x_ref[0]
    ts = x.shape[0]
    xn = _rms(x, pre_ref[...]).astype(jnp.bfloat16)
    row = lax.broadcasted_iota(jnp.int32, (ts, MIX_CC), 0)
    acc = jnp.zeros(x.shape, jnp.float32)
    for c in range(D_MODEL // MIX_CC):
        lo = c * MIX_CC
        b_gate = _dot(xn, win_ref[:, lo:lo + MIX_CC])
        c_gate = _dot(xn, win_ref[:, D_MODEL + lo:D_MODEL + lo + MIX_CC])
        h = _dot(xn, win_ref[:, 2 * D_MODEL + lo:2 * D_MODEL + lo + MIX_CC])
        u = c_gate * h
        prev1 = carry_ref[7:8, lo:lo + MIX_CC]
        prev2 = carry_ref[6:7, lo:lo + MIX_CC]
        u1 = jnp.where(row == 0, prev1, pltpu.roll(u, 1, axis=0))
        u2 = jnp.where(row == 0, prev2, jnp.where(row == 1, prev1, pltpu.roll(u, 2, axis=0)))
        carry_ref[:, lo:lo + MIX_CC] = u[ts - 8:, :]
        y = (ck_ref[0:1, lo:lo + MIX_CC] * u2 + ck_ref[1:2, lo:lo + MIX_CC] * u1
             + ck_ref[2:3, lo:lo + MIX_CC] * u)
        acc = acc + _dot((b_gate * y).astype(jnp.bfloat16), wout_ref[lo:lo + MIX_CC, :])
    o_ref[0] = x + _rms(acc, post_ref[...])


def _conv_mixer(x, pre_g, post_g, w_in, conv_k, w_out):
    b, s, d = x.shape
    return pl.pallas_call(
        _mix_kernel,
        name="conv_mixer",
        grid=(b, s // MIX_TS),
        in_specs=[
            pl.BlockSpec((1, MIX_TS, d), lambda i, j: (i, j, 0)),
            _resident((1, d)),
            _resident((1, d)),
            _resident((d, 3 * d)),
            _resident((3, d)),
            _resident((d, d)),
        ],
        out_specs=pl.BlockSpec((1, MIX_TS, d), lambda i, j: (i, j, 0)),
        out_shape=jax.ShapeDtypeStruct(x.shape, jnp.float32),
        scratch_shapes=[pltpu.VMEM((8, d), jnp.float32)],
        compiler_params=pltpu.CompilerParams(
            dimension_semantics=("arbitrary", "arbitrary"), vmem_limit_bytes=VMEM_LIMIT),
    )(x, pre_g.reshape(1, -1), post_g.reshape(1, -1), w_in, conv_k, w_out)


def _split3(a):
    hi = a.astype(jnp.bfloat16)
    r = a - hi.astype(jnp.float32)
    mid = r.astype(jnp.bfloat16)
    lo = (r - mid.astype(jnp.float32)).astype(jnp.bfloat16)
    return hi, mid, lo


def _kv_kernel(x_ref, g_ref, wkv_ref, wf_ref, fb_ref, k_ref, v_ref, c_ref, run_ref):
    @pl.when(pl.program_id(1) == 0)
    def _():
        run_ref[...] = jnp.zeros_like(run_ref)

    x = x_ref[0]
    ts = x.shape[0]
    xn = _rms(x, g_ref[...]).astype(jnp.bfloat16)
    for p in range(N_PAIRS):
        k_ref[0, p] = _dot(xn, wkv_ref[:, p * LANES:(p + 1) * LANES]).astype(jnp.bfloat16)
        v_ref[0, p] = _dot(
            xn, wkv_ref[:, D_MODEL + p * LANES:D_MODEL + (p + 1) * LANES]).astype(jnp.bfloat16)
    f_logit = _dot(xn, wf_ref[...]) + fb_ref[...]
    log_f = jnp.minimum(f_logit, 0.0) - jnp.log1p(jnp.exp(-jnp.abs(f_logit)))
    r = lax.broadcasted_iota(jnp.int32, (ts, ts), 0)
    q = lax.broadcasted_iota(jnp.int32, (ts, ts), 1)
    tri = (q <= r).astype(jnp.bfloat16)
    hi, mid, lo = _split3(log_f)
    csum = _dot(tri, hi) + _dot(tri, mid) + _dot(tri, lo) + run_ref[0:1, :]
    c_ref[0] = csum
    run_ref[...] = jnp.broadcast_to(csum[ts - 1:ts, :], run_ref.shape)


def _shared_kv(x, kv_g, w_kv, w_f, fb):
    b, s, d = x.shape
    kv_shape = jax.ShapeDtypeStruct((b, N_PAIRS, s, LANES), jnp.bfloat16)
    kv_spec = pl.BlockSpec((1, N_PAIRS, KV_TS, LANES), lambda i, j: (i, 0, j, 0))
    return pl.pallas_call(
        _kv_kernel,
        name="shared_kv",
        grid=(b, s // KV_TS),
        in_specs=[
            pl.BlockSpec((1, KV_TS, d), lambda i, j: (i, j, 0)),
            _resident((1, d)),
            _resident((d, 2 * d)),
            _resident((d, LANES)),
            _resident((1, LANES)),
        ],
        out_specs=[kv_spec, kv_spec,
                   pl.BlockSpec((1, KV_TS, LANES), lambda i, j: (i, j, 0))],
        out_shape=[kv_shape, kv_shape, jax.ShapeDtypeStruct((b, s, LANES), jnp.float32)],
        scratch_shapes=[pltpu.VMEM((8, LANES), jnp.float32)],
        compiler_params=pltpu.CompilerParams(
            dimension_semantics=("arbitrary", "arbitrary"), vmem_limit_bytes=VMEM_LIMIT),
    )(x, kv_g.reshape(1, -1), w_kv, w_f, fb)


def _qg_kernel(x_ref, g_ref, w_ref, q_ref, gate_ref):
    x = x_ref[0]
    xn = _rms(x, g_ref[...]).astype(jnp.bfloat16)
    scale = 1.0 / math.sqrt(HEAD_DIM)
    for p in range(N_PAIRS):
        q = _dot(xn, w_ref[:, p * LANES:(p + 1) * LANES])
        q_ref[0, p] = (q * scale).astype(jnp.bfloat16)
    gate_ref[0] = _dot(xn, w_ref[:, D_MODEL:])


def _qg_proj(x, g, w_qg):
    b, s, d = x.shape
    return pl.pallas_call(
        _qg_kernel,
        name="qg_proj",
        grid=(b, s // KV_TS),
        in_specs=[
            pl.BlockSpec((1, KV_TS, d), lambda i, j: (i, j, 0)),
            _resident((1, d)),
            _resident((d, 2 * d)),
        ],
        out_specs=[pl.BlockSpec((1, N_PAIRS, KV_TS, LANES), lambda i, j: (i, 0, j, 0)),
                   pl.BlockSpec((1, KV_TS, d), lambda i, j: (i, j, 0))],
        out_shape=[jax.ShapeDtypeStruct((b, N_PAIRS, s, LANES), jnp.bfloat16),
                   jax.ShapeDtypeStruct((b, s, d), jnp.float32)],
        compiler_params=pltpu.CompilerParams(
            dimension_semantics=("arbitrary", "arbitrary"), vmem_limit_bytes=VMEM_LIMIT),
    )(x, g.reshape(1, -1), w_qg)


def _attn_kernel(q_ref, k_ref, v_ref, ccol_ref, crow_ref, o_ref):
    pair = pl.program_id(1)
    t = ATT_T
    n_q = q_ref.shape[2] // t
    lane = lax.broadcasted_iota(jnp.int32, (1, LANES), 1)
    head_mask = [lane < HEAD_DIM, lane >= HEAD_DIM]
    r_idx = lax.broadcasted_iota(jnp.int32, (t, t), 0)
    c_idx = lax.broadcasted_iota(jnp.int32, (t, t), 1)
    causal = c_idx <= r_idx

    def q_tile(qi, _):
        q0 = pl.multiple_of(qi * t, t)
        q2 = q_ref[0, 0, pl.ds(q0, t), :]
        ccol2 = ccol_ref[0, pl.ds(q0, t), :]
        qs, ccols = [], []
        for j in range(2):
            qs.append(jnp.where(head_mask[j], q2, jnp.zeros_like(q2)))
            sel = lane == 2 * pair + j
            ccols.append(jnp.sum(jnp.where(sel, ccol2, 0.0), axis=-1, keepdims=True))

        def kv_step(kt, carry, masked):
            k0 = pl.multiple_of(kt * t, t)
            k2 = k_ref[0, 0, pl.ds(k0, t), :]
            v2 = v_ref[0, 0, pl.ds(k0, t), :]
            new = []
            for j in range(2):
                m, l, acc = carry[j]
                crow = crow_ref[0, 2 * pair + j, pl.ds(kt, 1), :]
                s = lax.dot_general(qs[j], k2, (((1,), (1,)), ((), ())),
                                    preferred_element_type=jnp.float32)
                s = s + ccols[j] - crow
                if masked:
                    s = jnp.where(causal, s, -jnp.inf)
                m_new = jnp.maximum(m, jnp.max(s, axis=-1, keepdims=True))
                alpha = jnp.exp(m - m_new)
                p = jnp.exp(s - m_new)
                l = alpha * l + jnp.sum(p, axis=-1, keepdims=True)
                acc = alpha * acc + _dot(p.astype(jnp.bfloat16), v2)
                new.append((m_new, l, acc))
            return tuple(new)

        init = tuple((jnp.full((t, 1), -jnp.inf, jnp.float32),
                      jnp.zeros((t, 1), jnp.float32),
                      jnp.zeros((t, LANES), jnp.float32)) for _ in range(2))
        carry = lax.fori_loop(0, qi, functools.partial(kv_step, masked=False), init)
        carry = kv_step(qi, carry, masked=True)
        (_, l0, a0), (_, l1, a1) = carry
        o_ref[0, pl.ds(q0, t), :] = jnp.where(head_mask[0], a0 / l0, a1 / l1)
        return 0

    lax.fori_loop(0, n_q, q_tile, 0)


def _attention(q, k, v, c_col, c_row):
    b, _, s, _ = q.shape
    qkv_spec = pl.BlockSpec((1, 1, s, LANES), lambda i, p: (i, p, 0, 0))
    return pl.pallas_call(
        _attn_kernel,
        name="fox_attention",
        grid=(b, N_PAIRS),
        in_specs=[
            qkv_spec, qkv_spec, qkv_spec,
            pl.BlockSpec((1, s, LANES), lambda i, p: (i, 0, 0)),
            pl.BlockSpec((1, N_HEADS, s // ATT_T, ATT_T), lambda i, p: (i, 0, 0, 0)),
        ],
        out_specs=pl.BlockSpec((1, s, LANES), lambda i, p: (i, 0, p)),
        out_shape=jax.ShapeDtypeStruct((b, s, D_MODEL), jnp.float32),
        compiler_params=pltpu.CompilerParams(
            dimension_semantics=("arbitrary", "arbitrary"), vmem_limit_bytes=VMEM_LIMIT),
    )(q, k, v, c_col, c_row)


def _attn_out_kernel(x_ref, o_ref, gate_ref, w_ref, post_ref, out_ref):
    m = _dot((jax.nn.sigmoid(gate_ref[...]) * o_ref[...]).astype(jnp.bfloat16), w_ref[...])
    out_ref[...] = x_ref[...] + _rms(m, post_ref[...])


def _attn_out(x2d, o2d, gate2d, w_o, post_g):
    t, d = x2d.shape
    row_spec = pl.BlockSpec((FFN_TM, d), lambda i: (i, 0))
    return pl.pallas_call(
        _attn_out_kernel,
        name="attn_out",
        grid=(t // FFN_TM,),
        in_specs=[row_spec, row_spec, row_spec, _resident((d, d)), _resident((1, d))],
        out_specs=row_spec,
        out_shape=jax.ShapeDtypeStruct(x2d.shape, jnp.float32),
        compiler_params=pltpu.CompilerParams(
            dimension_semantics=("arbitrary",), vmem_limit_bytes=VMEM_LIMIT),
    )(x2d, o2d, gate2d, w_o, post_g.reshape(1, -1))


def kernel(x, ffn1_pre_g, ffn1_post_g, ffn1_w_in, ffn1_w_out, mix_pre_g, mix_post_g,
           ffn2_pre_g, ffn2_post_g, ffn2_w_in, ffn2_w_out, conv_w_in, conv_k, conv_w_out,
           kv_g, kv_w, forget_b, attn_w_qg, attn_w_o):
    b, s, d = x.shape
    bf = jnp.bfloat16

    def ffn(h, pre, post, w_in, w_out):
        return _ffn(h.reshape(b * s, d), pre, post, w_in.astype(bf),
                    w_out.astype(bf)).reshape(b, s, d)

    h = ffn(x, ffn1_pre_g[0], ffn1_post_g[0], ffn1_w_in[0], ffn1_w_out[0])
    h = _conv_mixer(h, mix_pre_g[0], mix_post_g[0], conv_w_in[0].astype(bf), conv_k[0],
                    conv_w_out[0].astype(bf))
    h = ffn(h, ffn2_pre_g[0], ffn2_post_g[0], ffn2_w_in[0], ffn2_w_out[0])

    w_f = jnp.pad(kv_w[:, 2 * d:], ((0, 0), (0, LANES - N_HEADS))).astype(bf)
    fb = jnp.pad(forget_b, (0, LANES - N_HEADS)).reshape(1, LANES)
    k, v, c = _shared_kv(h, kv_g, kv_w[:, :2 * d].astype(bf), w_f, fb)
    c_row = c[:, :, :N_HEADS].transpose(0, 2, 1).reshape(b, N_HEADS, s // ATT_T, ATT_T)

    h = ffn(h, ffn1_pre_g[1], ffn1_post_g[1], ffn1_w_in[1], ffn1_w_out[1])
    q, gate = _qg_proj(h, mix_pre_g[1], attn_w_qg[0].astype(bf))
    o = _attention(q, k, v, c, c_row)
    h = _attn_out(h.reshape(b * s, d), o.reshape(b * s, d), gate.reshape(b * s, d),
                  attn_w_o[0].astype(bf), mix_post_g[1]).reshape(b, s, d)
    h = ffn(h, ffn2_pre_g[1], ffn2_post_g[1], ffn2_w_in[1], ffn2_w_out[1])
    return h
```

```python
import functools
import math

import jax
import jax.numpy as jnp
from jax import lax
from jax.experimental import pallas as pl
from jax.experimental.pallas import tpu as pltpu

D_MODEL = 1024
N_HEADS = 16
HEAD_DIM = D_MODEL // N_HEADS
N_PAIRS = N_HEADS // 2
D_FF = 2816
RMS_EPS = 1e-6
LANES = 128
VMEM_LIMIT = 56 * 1024 * 1024

FFN_TM = 512
FFN_FC = 256
MIX_TS = 512
MIX_CC = 256
KV_TS = 512
ATT_T = 256
ONES_ROWS = 16
SOFTMAX_LAG = 3
VALUE_LAG = 6

BIAS_MID, BIAS_LO, BIAS_Q = 16, 32, 64


def _rms(x, g):
    return x * lax.rsqrt(jnp.mean(x * x, axis=-1, keepdims=True) + RMS_EPS) * g


def _dot(a, b):
    return jnp.dot(a, b, preferred_element_type=jnp.float32)


def _dot_nt(a, b):
    return lax.dot_general(a, b, (((1,), (1,)), ((), ())), preferred_element_type=jnp.float32)


def _resident(shape):
    nd = len(shape)
    return pl.BlockSpec(shape, lambda *_: (0,) * nd, pipeline_mode=pl.Buffered(1))


def _ffn_kernel(x_ref, pre_ref, post_ref, win_ref, wout_ref, o_ref):
    x = x_ref[...]
    xn = _rms(x, pre_ref[...]).astype(jnp.bfloat16)
    acc = jnp.zeros(x.shape, jnp.float32)
    for c in range(D_FF // FFN_FC):
        lo = c * FFN_FC
        gate = _dot(xn, win_ref[:, lo:lo + FFN_FC])
        up = _dot(xn, win_ref[:, D_FF + lo:D_FF + lo + FFN_FC])
        h = (gate * jax.nn.sigmoid(gate) * up).astype(jnp.bfloat16)
        acc = acc + _dot(h, wout_ref[lo:lo + FFN_FC, :])
    o_ref[...] = x + 0.5 * _rms(acc, post_ref[...])


def _ffn(x2d, pre_g, post_g, w_in, w_out):
    t = x2d.shape[0]
    return pl.pallas_call(
        _ffn_kernel,
        name="ffn",
        grid=(t // FFN_TM,),
        in_specs=[
            pl.BlockSpec((FFN_TM, D_MODEL), lambda i: (i, 0)),
            _resident((1, D_MODEL)),
            _resident((1, D_MODEL)),
            _resident((D_MODEL, 2 * D_FF)),
            _resident((D_FF, D_MODEL)),
        ],
        out_specs=pl.BlockSpec((FFN_TM, D_MODEL), lambda i: (i, 0)),
        out_shape=jax.ShapeDtypeStruct(x2d.shape, jnp.float32),
        compiler_params=pltpu.CompilerParams(
            dimension_semantics=("arbitrary",), vmem_limit_bytes=VMEM_LIMIT),
    )(x2d, pre_g.reshape(1, -1), post_g.reshape(1, -1), w_in, w_out)


def _mix_kernel(x_ref, pre_ref, post_ref, win_ref, ck_ref, wout_ref, o_ref, carry_ref):
    @pl.when(pl.program_id(1) == 0)
    def _():
        carry_ref[...] = jnp.zeros_like(carry_ref)

    x = x_ref[0]
    ts = x.shape[0]
    xn = _rms(x, pre_ref[...]).astype(jnp.bfloat16)
    row = lax.broadcasted_iota(jnp.int32, (ts, MIX_CC), 0)
    acc = jnp.zeros(x.shape, jnp.float32)
    for c in range(D_MODEL // MIX_CC):
        lo = c * MIX_CC
        b_gate = _dot(xn, win_ref[:, lo:lo + MIX_CC])
        c_gate = _dot(xn, win_ref[:, D_MODEL + lo:D_MODEL + lo + MIX_CC])
        h = _dot(xn, win_ref[:, 2 * D_MODEL + lo:2 * D_MODEL + lo + MIX_CC])
        u = c_gate * h
        prev1 = carry_ref[7:8, lo:lo + MIX_CC]
        prev2 = carry_ref[6:7, lo:lo + MIX_CC]
        u1 = jnp.where(row == 0, prev1, pltpu.roll(u, 1, axis=0))
        u2 = jnp.where(row == 0, prev2, jnp.where(row == 1, prev1, pltpu.roll(u, 2, axis=0)))
        carry_ref[:, lo:lo + MIX_CC] = u[ts - 8:, :]
        y = (ck_ref[0:1, lo:lo + MIX_CC] * u2 + ck_ref[1:2, lo:lo + MIX_CC] * u1
             + ck_ref[2:3, lo:lo + MIX_CC] * u)
        acc = acc + _dot((b_gate * y).astype(jnp.bfloat16), wout_ref[lo:lo + MIX_CC, :])
    o_ref[0] = x + _rms(acc, post_ref[...])


def _conv_mixer(x, pre_g, post_g, w_in, conv_k, w_out):
    b, s, d = x.shape
    return pl.pallas_call(
        _mix_kernel,
        name="conv_mixer",
        grid=(b, s // MIX_TS),
        in_specs=[
            pl.BlockSpec((1, MIX_TS, d), lambda i, j: (i, j, 0)),
            _resident((1, d)),
            _resident((1, d)),
            _resident((d, 3 * d)),
            _resident((3, d)),
            _resident((d, d)),
        ],
        out_specs=pl.BlockSpec((1, MIX_TS, d), lambda i, j: (i, j, 0)),
        out_shape=jax.ShapeDtypeStruct(x.shape, jnp.float32),
        scratch_shapes=[pltpu.VMEM((8, d), jnp.float32)],
        compiler_params=pltpu.CompilerParams(
            dimension_semantics=("arbitrary", "arbitrary"), vmem_limit_bytes=VMEM_LIMIT),
    )(x, pre_g.reshape(1, -1), post_g.reshape(1, -1), w_in, conv_k, w_out)


def _split3(a):
    hi = a.astype(jnp.bfloat16)
    r = a - hi.astype(jnp.float32)
    mid = r.astype(jnp.bfloat16)
    lo = (r - mid.astype(jnp.float32)).astype(jnp.bfloat16)
    return hi, mid, lo


def _kv_kernel(x_ref, g_ref, wkv_ref, wf_ref, fb_ref, k_ref, vt_ref, kb_ref, qb_ref, run_ref):
    @pl.when(pl.program_id(1) == 0)
    def _():
        run_ref[...] = jnp.zeros_like(run_ref)

    x = x_ref[0]
    ts = x.shape[0]
    xn = _rms(x, g_ref[...]).astype(jnp.bfloat16)
    for p in range(N_PAIRS):
        k_ref[0, p] = _dot(xn, wkv_ref[:, p * LANES:(p + 1) * LANES]).astype(jnp.bfloat16)
        v = _dot(xn, wkv_ref[:, D_MODEL + p * LANES:D_MODEL + (p + 1) * LANES])
        for r in range(ts // ATT_T):
            vt_ref[0, p, r] = v[r * ATT_T:(r + 1) * ATT_T, :].T.astype(jnp.bfloat16)
    f_logit = _dot(xn, wf_ref[...]) + fb_ref[...]
    log_f = jnp.minimum(f_logit, 0.0) - jnp.log1p(jnp.exp(-jnp.abs(f_logit)))
    r_idx = lax.broadcasted_iota(jnp.int32, (ts, ts), 0)
    c_idx = lax.broadcasted_iota(jnp.int32, (ts, ts), 1)
    tri = (c_idx <= r_idx).astype(jnp.bfloat16)
    hi, mid, lo = _split3(log_f)
    csum = _dot(tri, hi) + _dot(tri, mid) + _dot(tri, lo) + run_ref[0:1, :]
    run_ref[...] = jnp.broadcast_to(csum[ts - 1:ts, :], run_ref.shape)
    lane = lax.broadcasted_iota(jnp.int32, (1, LANES), 1)
    hi, mid, lo = (part.astype(jnp.float32)
                   for part in _split3(jnp.where(lane < N_HEADS, csum, 0.0)))
    key_bias = (jnp.where(lane >= BIAS_Q, 1.0, -hi) - pltpu.roll(mid, BIAS_MID, axis=1)
                - pltpu.roll(lo, BIAS_LO, axis=1))
    query_bias = (pltpu.roll(hi, BIAS_Q, axis=1) + pltpu.roll(mid, BIAS_Q + BIAS_MID, axis=1)
                  + pltpu.roll(lo, BIAS_Q + BIAS_LO, axis=1))
    kb_ref[0] = key_bias.astype(jnp.bfloat16)
    qb_ref[0] = query_bias.astype(jnp.bfloat16)


def _shared_kv(x, kv_g, w_kv, w_f, fb):
    b, s, d = x.shape
    bias_shape = jax.ShapeDtypeStruct((b, s, LANES), jnp.bfloat16)
    bias_spec = pl.BlockSpec((1, KV_TS, LANES), lambda i, j: (i, j, 0))
    n_sub = KV_TS // ATT_T
    return pl.pallas_call(
        _kv_kernel,
        name="shared_kv",
        grid=(b, s // KV_TS),
        in_specs=[
            pl.BlockSpec((1, KV_TS, d), lambda i, j: (i, j, 0)),
            _resident((1, d)),
            _resident((d, 2 * d)),
            _resident((d, LANES)),
            _resident((1, LANES)),
        ],
        out_specs=[
            pl.BlockSpec((1, N_PAIRS, KV_TS, LANES), lambda i, j: (i, 0, j, 0)),
            pl.BlockSpec((1, N_PAIRS, n_sub, LANES, ATT_T), lambda i, j: (i, 0, j, 0, 0)),
            bias_spec, bias_spec],
        out_shape=[
            jax.ShapeDtypeStruct((b, N_PAIRS, s, LANES), jnp.bfloat16),
            jax.ShapeDtypeStruct((b, N_PAIRS, s // ATT_T, LANES, ATT_T), jnp.bfloat16),
            bias_shape, bias_shape],
        scratch_shapes=[pltpu.VMEM((8, LANES), jnp.float32)],
        compiler_params=pltpu.CompilerParams(
            dimension_semantics=("arbitrary", "arbitrary"), vmem_limit_bytes=VMEM_LIMIT),
    )(x, kv_g.reshape(1, -1), w_kv, w_f, fb)


def _qg_kernel(x_ref, g_ref, w_ref, q_ref, gate_ref):
    x = x_ref[0]
    xn = _rms(x, g_ref[...]).astype(jnp.bfloat16)
    scale = 1.0 / math.sqrt(HEAD_DIM)
    for p in range(N_PAIRS):
        q = _dot(xn, w_ref[:, p * LANES:(p + 1) * LANES])
        q_ref[0, p] = (q * scale).astype(jnp.bfloat16)
    gate_ref[0] = _dot(xn, w_ref[:, D_MODEL:])


def _qg_proj(x, g, w_qg):
    b, s, d = x.shape
    return pl.pallas_call(
        _qg_kernel,
        name="qg_proj",
        grid=(b, s // KV_TS),
        in_specs=[
            pl.BlockSpec((1, KV_TS, d), lambda i, j: (i, j, 0)),
            _resident((1, d)),
            _resident((d, 2 * d)),
        ],
        out_specs=[pl.BlockSpec((1, N_PAIRS, KV_TS, LANES), lambda i, j: (i, 0, j, 0)),
                   pl.BlockSpec((1, KV_TS, d), lambda i, j: (i, j, 0))],
        out_shape=[jax.ShapeDtypeStruct((b, N_PAIRS, s, LANES), jnp.bfloat16),
                   jax.ShapeDtypeStruct((b, s, d), jnp.float32)],
        compiler_params=pltpu.CompilerParams(
            dimension_semantics=("arbitrary", "arbitrary"), vmem_limit_bytes=VMEM_LIMIT),
    )(x, g.reshape(1, -1), w_qg)


def _attn_kernel(q_ref, qb_ref, k_ref, kb_ref, vt_ref, o_ref, qa_ref, m_ref, acc_ref):
    pair = pl.program_id(1)
    t = ATT_T
    n_q = q_ref.shape[2] // t
    lane = lax.broadcasted_iota(jnp.int32, (1, LANES), 1)
    key_idx = lax.broadcasted_iota(jnp.int32, (t, t), 0)
    query_idx = lax.broadcasted_iota(jnp.int32, (t, t), 1)
    causal = key_idx <= query_idx
    ones_rows = jnp.ones((ONES_ROWS, t), jnp.bfloat16)
    head_mask, bias_mask, bias_ones = [], [], []
    for j in range(2):
        h = 2 * pair + j
        head_mask.append((lane >= j * HEAD_DIM) & (lane < (j + 1) * HEAD_DIM))
        bias_mask.append((lane == BIAS_Q + h) | (lane == BIAS_Q + BIAS_MID + h)
                         | (lane == BIAS_Q + BIAS_LO + h))
        bias_ones.append(jnp.where((lane == h) | (lane == BIAS_MID + h) | (lane == BIAS_LO + h),
                                   1.0, 0.0).astype(jnp.bfloat16))

    for qi in range(n_q):
        q2 = q_ref[0, 0, qi * t:(qi + 1) * t, :]
        qb = qb_ref[0, qi * t:(qi + 1) * t, :]
        for j in range(2):
            qa_ref[qi, j] = jnp.concatenate(
                [jnp.where(head_mask[j], q2, jnp.zeros_like(q2)),
                 jnp.where(bias_mask[j], qb, bias_ones[j])], axis=1)

    chains = [(kt, qi, j) for kt in range(n_q) for qi in range(kt, n_q) for j in range(2)]
    scores, probs, o_t = {}, {}, {}

    def score_stage(i):
        kt, qi, j = chains[i]
        ka = jnp.concatenate([k_ref[0, 0, kt * t:(kt + 1) * t, :],
                              kb_ref[0, kt * t:(kt + 1) * t, :]], axis=1)
        scores[i] = _dot_nt(ka, qa_ref[qi, j])

    def softmax_stage(i):
        kt, qi, j = chains[i]
        s = scores.pop(i)
        if qi == kt:
            s = jnp.where(causal, s, -jnp.inf)
        m_new = jnp.max(s, axis=0, keepdims=True)
        alpha = None
        if kt > 0:
            m_old = m_ref[qi, j]
            m_new = jnp.maximum(m_old, m_new)
            alpha = jnp.exp(m_old - m_new)
        if qi > kt:
            m_ref[qi, j] = m_new
        probs[i] = (jnp.exp(s - m_new).astype(jnp.bfloat16), alpha)

    def value_stage(i):
        kt, qi, j = chains[i]
        p, alpha = probs.pop(i)
        v_aug = jnp.concatenate(
            [vt_ref[0, 0, kt, j * HEAD_DIM:(j + 1) * HEAD_DIM, :], ones_rows], axis=0)
        acc = _dot(v_aug, p)
        if kt > 0:
            acc = alpha * acc_ref[qi, j] + acc
        if qi > kt:
            acc_ref[qi, j] = acc
            return
        o_t[j] = acc[:HEAD_DIM] / acc[HEAD_DIM:HEAD_DIM + 1]
        if j == 1:
            o_ref[0, qi * t:(qi + 1) * t, :] = jnp.concatenate([o_t.pop(0), o_t.pop(1)], axis=0).T

    n = len(chains)
    for i in range(n + VALUE_LAG):
        if i < n:
            score_stage(i)
        if SOFTMAX_LAG <= i < n + SOFTMAX_LAG:
            softmax_stage(i - SOFTMAX_LAG)
        if i >= VALUE_LAG:
            value_stage(i - VALUE_LAG)


def _attention(q, q_bias, k, k_bias, v_t):
    b, _, s, _ = q.shape
    n_q = s // ATT_T
    qk_spec =pl.BlockSpec((1, 1, s, LANES), lambda i, p: (i, p, 0, 0))
    bias_spec = pl.BlockSpec((1, s, LANES), lambda i, p: (i, 0, 0))
    return pl.pallas_call(
        _attn_kernel,
        name="fox_attention",
        grid=(b, N_PAIRS),
        in_specs=[
            qk_spec, bias_spec, qk_spec, bias_spec,
            pl.BlockSpec((1, 1, s // ATT_T, LANES, ATT_T), lambda i, p: (i, p, 0, 0, 0)),
        ],
        out_specs=pl.BlockSpec((1, s, LANES), lambda i, p: (i, 0, p)),
        out_shape=jax.ShapeDtypeStruct((b, s, D_MODEL), jnp.float32),
        scratch_shapes=[
            pltpu.VMEM((n_q, 2, ATT_T, 2 * LANES), jnp.bfloat16),
            pltpu.VMEM((n_q, 2, 1, ATT_T), jnp.float32),
            pltpu.VMEM((n_q, 2, HEAD_DIM + ONES_ROWS, ATT_T), jnp.float32),
        ],
        compiler_params=pltpu.CompilerParams(
            dimension_semantics=("arbitrary", "arbitrary"), vmem_limit_bytes=VMEM_LIMIT),
    )(q, q_bias, k, k_bias, v_t)


def _attn_out_kernel(x_ref, o_ref, gate_ref, w_ref, post_ref, out_ref):
    m = _dot((jax.nn.sigmoid(gate_ref[...]) * o_ref[...]).astype(jnp.bfloat16), w_ref[...])
    out_ref[...] = x_ref[...] + _rms(m, post_ref[...])


def _attn_out(x2d, o2d, gate2d, w_o, post_g):
    t, d = x2d.shape
    row_spec = pl.BlockSpec((FFN_TM, d), lambda i: (i, 0))
    return pl.pallas_call(
        _attn_out_kernel,
        name="attn_out",
        grid=(t // FFN_TM,),
        in_specs=[row_spec, row_spec, row_spec, _resident((d, d)), _resident((1, d))],
        out_specs=row_spec,
        out_shape=jax.ShapeDtypeStruct(x2d.shape, jnp.float32),
        compiler_params=pltpu.CompilerParams(
            dimension_semantics=("arbitrary",), vmem_limit_bytes=VMEM_LIMIT),
    )(x2d, o2d, gate2d, w_o, post_g.reshape(1, -1))


def kernel(x, ffn1_pre_g, ffn1_post_g, ffn1_w_in, ffn1_w_out, mix_pre_g, mix_post_g,
           ffn2_pre_g, ffn2_post_g, ffn2_w_in, ffn2_w_out, conv_w_in, conv_k, conv_w_out,
           kv_g, kv_w, forget_b, attn_w_qg, attn_w_o):
    b, s, d = x.shape
    bf = jnp.bfloat16

    def ffn(h, pre, post, w_in, w_out):
        return _ffn(h.reshape(b * s, d), pre, post, w_in.astype(bf),
                    w_out.astype(bf)).reshape(b, s, d)

    h = ffn(x, ffn1_pre_g[0], ffn1_post_g[0], ffn1_w_in[0], ffn1_w_out[0])
    h = _conv_mixer(h, mix_pre_g[0], mix_post_g[0], conv_w_in[0].astype(bf), conv_k[0],
                    conv_w_out[0].astype(bf))
    h = ffn(h, ffn2_pre_g[0], ffn2_post_g[0], ffn2_w_in[0], ffn2_w_out[0])

    w_f = jnp.pad(kv_w[:, 2 * d:], ((0, 0), (0, LANES - N_HEADS))).astype(bf)
    fb = jnp.pad(forget_b, (0, LANES - N_HEADS)).reshape(1, LANES)
    k, v_t, k_bias, q_bias = _shared_kv(h, kv_g, kv_w[:, :2 * d].astype(bf), w_f, fb)

    h = ffn(h, ffn1_pre_g[1], ffn1_post_g[1], ffn1_w_in[1], ffn1_w_out[1])
    q, gate = _qg_proj(h, mix_pre_g[1], attn_w_qg[0].astype(bf))
    o = _attention(q, q_bias, k, k_bias, v_t)
    h = _attn_out(h.reshape(b * s, d), o.reshape(b * s, d), gate.reshape(b * s, d),
                  attn_w_o[0].astype(bf), mix_post_g[1]).reshape(b, s, d)
    h = ffn(h, ffn2_pre_g[1], ffn2_post_g[1], ffn2_w_in[1], ffn2_w_out[1])
    return h
```

```python
import functools
import math

import jax
import jax.numpy as jnp
from jax import lax
from jax.experimental import pallas as pl
from jax.experimental.pallas import tpu as pltpu

D_MODEL = 1024
N_HEADS = 16
HEAD_DIM = D_MODEL // N_HEADS
N_PAIRS = N_HEADS // 2
D_FF = 2816
RMS_EPS = 1e-6
LANES = 128
MXU_COLS = 256
VMEM_LIMIT = 56 * 1024 * 1024

FFN_TM = 1024
FFN_SUB = 512
FFN_FC = 256
MIX_TS = 512
MIX_CC = 256
KV_TS = 512
ATT_T = 256
ONES_ROWS = 16
LOG2_E = math.log2(math.e)
SOFTMAX_LAG = 3
VALUE_LAG = 6

BIAS_MID, BIAS_LO, BIAS_Q = 16, 32, 64


def _rms(x, g):
    return x * lax.rsqrt(jnp.mean(x * x, axis=-1, keepdims=True) + RMS_EPS) * g


def _dot(a, b):
    return jnp.dot(a, b, preferred_element_type=jnp.float32)


def _dot_nt(a, b):
    return lax.dot_general(a, b, (((1,), (1,)), ((), ())), preferred_element_type=jnp.float32)


def _resident(shape):
    nd = len(shape)
    return pl.BlockSpec(shape, lambda *_: (0,) * nd, pipeline_mode=pl.Buffered(1))


def _ffn_kernel(x_ref, pre_ref, post_ref, win_ref, wout_ref, o_ref):
    n_sub = x_ref.shape[0] // FFN_SUB
    n_chunks = D_FF // FFN_FC
    xn, acc = {}, {}

    def rows(r):
        return slice(r * FFN_SUB, (r + 1) * FFN_SUB)

    def prologue(r):
        xn[r] = _rms(x_ref[rows(r), :], pre_ref[...]).astype(jnp.bfloat16)
        acc[r] = jnp.zeros((FFN_SUB, D_MODEL), jnp.float32)

    def epilogue(r):
        o_ref[rows(r), :] = x_ref[rows(r), :] + 0.5 * _rms(acc.pop(r), post_ref[...])

    prologue(0)
    for r in range(n_sub):
        for c in range(n_chunks):
            lo = c * FFN_FC
            gate = _dot(xn[r], win_ref[:, lo:lo + FFN_FC])
            up = _dot(xn[r], win_ref[:, D_FF + lo:D_FF + lo + FFN_FC])
            h = (gate * jax.nn.sigmoid(gate) * up).astype(jnp.bfloat16)
            acc[r] = acc[r] + _dot(h, wout_ref[lo:lo + FFN_FC, :])
            if c == 0 and r + 1 < n_sub:
                prologue(r + 1)
            if c == 1 and r >= 1:
                epilogue(r - 1)
    epilogue(n_sub - 1)


def _ffn(x2d, pre_g, post_g, w_in, w_out):
    t = x2d.shape[0]
    return pl.pallas_call(
        _ffn_kernel,
        name="ffn",
        grid=(t // FFN_TM,),
        in_specs=[
            pl.BlockSpec((FFN_TM, D_MODEL), lambda i: (i, 0)),
            _resident((1, D_MODEL)),
            _resident((1, D_MODEL)),
            _resident((D_MODEL, 2 * D_FF)),
            _resident((D_FF, D_MODEL)),
        ],
        out_specs=pl.BlockSpec((FFN_TM, D_MODEL), lambda i: (i, 0)),
        out_shape=jax.ShapeDtypeStruct(x2d.shape, jnp.float32),
        compiler_params=pltpu.CompilerParams(
            dimension_semantics=("arbitrary",), vmem_limit_bytes=VMEM_LIMIT),
    )(x2d, pre_g.reshape(1, -1), post_g.reshape(1, -1), w_in, w_out)


def _mix_kernel(x_ref, pre_ref, post_ref, win_ref, ck_ref, wout_ref, o_ref, carry_ref):
    @pl.when(pl.program_id(1) == 0)
    def _():
        carry_ref[...] = jnp.zeros_like(carry_ref)

    x = x_ref[0]
    ts = x.shape[0]
    xn = _rms(x, pre_ref[...]).astype(jnp.bfloat16)
    row = lax.broadcasted_iota(jnp.int32, (ts, MIX_CC), 0)
    acc = jnp.zeros(x.shape, jnp.float32)
    n_chunks = D_MODEL // MIX_CC
    proj = {}
    for step in range(n_chunks + 1):
        if step < n_chunks:
            lo = step * MIX_CC
            proj[step] = tuple(_dot(xn, win_ref[:, part * D_MODEL + lo:part * D_MODEL + lo + MIX_CC])
                               for part in range(3))
        if step == 0:
            continue
        c = step - 1
        lo = c * MIX_CC
        b_gate, c_gate, h = proj.pop(c)
        u = c_gate * h
        prev1 = carry_ref[7:8, lo:lo + MIX_CC]
        prev2 = carry_ref[6:7, lo:lo + MIX_CC]
        u1 = jnp.where(row == 0, prev1, pltpu.roll(u, 1, axis=0))
        u2 = jnp.where(row == 0, prev2, jnp.where(row == 1, prev1, pltpu.roll(u, 2, axis=0)))
        carry_ref[:, lo:lo + MIX_CC] = u[ts - 8:, :]
        y = (ck_ref[0:1, lo:lo + MIX_CC] * u2 + ck_ref[1:2, lo:lo + MIX_CC] * u1
             + ck_ref[2:3, lo:lo + MIX_CC] * u)
        acc = acc + _dot((b_gate * y).astype(jnp.bfloat16), wout_ref[lo:lo + MIX_CC, :])
    o_ref[0] = x + _rms(acc, post_ref[...])


def _conv_mixer(x, pre_g, post_g, w_in, conv_k, w_out):
    b, s, d = x.shape
    return pl.pallas_call(
        _mix_kernel,
        name="conv_mixer",
        grid=(b, s // MIX_TS),
        in_specs=[
            pl.BlockSpec((1, MIX_TS, d), lambda i, j: (i, j, 0)),
            _resident((1, d)),
            _resident((1, d)),
            _resident((d, 3 * d)),
            _resident((3, d)),
            _resident((d, d)),
        ],
        out_specs=pl.BlockSpec((1, MIX_TS, d), lambda i, j: (i, j, 0)),
        out_shape=jax.ShapeDtypeStruct(x.shape, jnp.float32),
        scratch_shapes=[pltpu.VMEM((8, d), jnp.float32)],
        compiler_params=pltpu.CompilerParams(
            dimension_semantics=("arbitrary", "arbitrary"), vmem_limit_bytes=VMEM_LIMIT),
    )(x, pre_g.reshape(1, -1), post_g.reshape(1, -1), w_in, conv_k, w_out)


def _split3(a):
    hi = a.astype(jnp.bfloat16)
    r = a - hi.astype(jnp.float32)
    mid = r.astype(jnp.bfloat16)
    lo = (r - mid.astype(jnp.float32)).astype(jnp.bfloat16)
    return hi, mid, lo


def _kv_kernel(x_ref, g_ref, wkv_ref, wf_ref, fb_ref, k_ref, vt_ref, kb_ref, qb_ref,
               run_ref, tri_ref):
    x = x_ref[0]
    ts = x.shape[0]

    @pl.when(pl.program_id(1) == 0)
    def _():
        run_ref[...] = jnp.zeros_like(run_ref)

    @pl.when((pl.program_id(0) == 0) & (pl.program_id(1) == 0))
    def _():
        r_idx = lax.broadcasted_iota(jnp.int32, (ts, ts), 0)
        c_idx = lax.broadcasted_iota(jnp.int32, (ts, ts), 1)
        tri_ref[...] = (c_idx <= r_idx).astype(jnp.bfloat16)

    xn = _rms(x, g_ref[...]).astype(jnp.bfloat16)
    for c in range(D_MODEL // MXU_COLS):
        k2 = _dot(xn, wkv_ref[:, c * MXU_COLS:(c + 1) * MXU_COLS])
        v2 = _dot(xn, wkv_ref[:, D_MODEL + c * MXU_COLS:D_MODEL + (c + 1) * MXU_COLS])
        for half in range(MXU_COLS // LANES):
            p = c * (MXU_COLS // LANES) + half
            k_ref[0, p] = k2[:, half * LANES:(half + 1) * LANES].astype(jnp.bfloat16)
            for r in range(ts // ATT_T):
                vt_ref[0, p, r] = v2[r * ATT_T:(r + 1) * ATT_T,
                                     half * LANES:(half + 1) * LANES].T.astype(jnp.bfloat16)
    lane = lax.broadcasted_iota(jnp.int32, (1, LANES), 1)
    is_head = lane < N_HEADS
    f_logit = _dot(xn, wf_ref[...]) + fb_ref[...]
    log_f = jnp.minimum(f_logit, 0.0) - jnp.log1p(jnp.exp(-jnp.abs(f_logit)))
    hi, mid, lo = (part.astype(jnp.float32) for part in _split3(jnp.where(is_head, log_f, 0.0)))
    packed = hi + pltpu.roll(mid, BIAS_MID, axis=1) + pltpu.roll(lo, BIAS_LO, axis=1)
    sums = _dot(tri_ref[...], packed.astype(jnp.bfloat16))
    sums = (sums + pltpu.roll(sums, LANES - BIAS_MID, axis=1)
            + pltpu.roll(sums, LANES - BIAS_LO, axis=1))
    csum = jnp.where(is_head, sums + run_ref[0:1, :], 0.0)
    run_ref[...] = jnp.broadcast_to(csum[ts - 1:ts, :], run_ref.shape)
    hi, mid, lo = (part.astype(jnp.float32) for part in _split3(csum * LOG2_E))
    key_bias = (jnp.where(lane >= BIAS_Q, 1.0, -hi) - pltpu.roll(mid, BIAS_MID, axis=1)
                - pltpu.roll(lo, BIAS_LO, axis=1))
    query_bias = (pltpu.roll(hi, BIAS_Q, axis=1) + pltpu.roll(mid, BIAS_Q + BIAS_MID, axis=1)
                  + pltpu.roll(lo, BIAS_Q + BIAS_LO, axis=1))
    kb_ref[0] = key_bias.astype(jnp.bfloat16)
    qb_ref[0] = query_bias.astype(jnp.bfloat16)


def _shared_kv(x, kv_g, w_kv, w_f, fb):
    b, s, d = x.shape
    bias_shape = jax.ShapeDtypeStruct((b, s, LANES), jnp.bfloat16)
    bias_spec = pl.BlockSpec((1, KV_TS, LANES), lambda i, j: (i, j, 0))
    n_sub = KV_TS // ATT_T
    return pl.pallas_call(
        _kv_kernel,
        name="shared_kv",
        grid=(b, s // KV_TS),
        in_specs=[
            pl.BlockSpec((1, KV_TS, d), lambda i, j: (i, j, 0)),
            _resident((1, d)),
            _resident((d, 2 * d)),
            _resident((d, LANES)),
            _resident((1, LANES)),
        ],
        out_specs=[
            pl.BlockSpec((1, N_PAIRS, KV_TS, LANES), lambda i, j: (i, 0, j, 0)),
            pl.BlockSpec((1, N_PAIRS, n_sub, LANES, ATT_T), lambda i, j: (i, 0, j, 0, 0)),
            bias_spec, bias_spec],
        out_shape=[
            jax.ShapeDtypeStruct((b, N_PAIRS, s, LANES), jnp.bfloat16),
            jax.ShapeDtypeStruct((b, N_PAIRS, s // ATT_T, LANES, ATT_T), jnp.bfloat16),
            bias_shape, bias_shape],
        scratch_shapes=[pltpu.VMEM((8, LANES), jnp.float32),
                        pltpu.VMEM((KV_TS, KV_TS), jnp.bfloat16)],
        compiler_params=pltpu.CompilerParams(
            dimension_semantics=("arbitrary", "arbitrary"), vmem_limit_bytes=VMEM_LIMIT),
    )(x, kv_g.reshape(1, -1), w_kv, w_f, fb)


def _qg_kernel(x_ref, g_ref, w_ref, q_ref, gate_ref):
    x = x_ref[0]
    xn = _rms(x, g_ref[...]).astype(jnp.bfloat16)
    scale = LOG2_E / math.sqrt(HEAD_DIM)
    for c in range(D_MODEL // MXU_COLS):
        q2 = _dot(xn, w_ref[:, c * MXU_COLS:(c + 1) * MXU_COLS]) * scale
        for half in range(MXU_COLS // LANES):
            q_ref[0, c * (MXU_COLS // LANES) + half] = (
                q2[:, half * LANES:(half + 1) * LANES].astype(jnp.bfloat16))
    gate_ref[0] = _dot(xn, w_ref[:, D_MODEL:])


def _qg_proj(x, g, w_qg):
    b, s, d = x.shape
    return pl.pallas_call(
        _qg_kernel,
        name="qg_proj",
        grid=(b, s // KV_TS),
        in_specs=[
            pl.BlockSpec((1, KV_TS, d), lambda i, j: (i, j, 0)),
            _resident((1, d)),
            _resident((d, 2 * d)),
        ],
        out_specs=[pl.BlockSpec((1, N_PAIRS, KV_TS, LANES), lambda i, j: (i, 0, j, 0)),
                   pl.BlockSpec((1, KV_TS, d), lambda i, j: (i, j, 0))],
        out_shape=[jax.ShapeDtypeStruct((b, N_PAIRS, s, LANES), jnp.bfloat16),
                   jax.ShapeDtypeStruct((b, s, d), jnp.float32)],
        compiler_params=pltpu.CompilerParams(
            dimension_semantics=("arbitrary", "arbitrary"), vmem_limit_bytes=VMEM_LIMIT),
    )(x, g.reshape(1, -1), w_qg)


def _attn_kernel(q_ref, qb_ref, k_ref, kb_ref, vt_ref, gate_ref, o_ref, qa_ref, m_ref, acc_ref):
    pair = pl.program_id(1)
    t = ATT_T
    n_q = q_ref.shape[2] // t
    lane = lax.broadcasted_iota(jnp.int32, (1, LANES), 1)
    key_idx = lax.broadcasted_iota(jnp.int32, (t, t), 0)
    query_idx = lax.broadcasted_iota(jnp.int32, (t, t), 1)
    causal = key_idx <= query_idx
    ones_rows = jnp.ones((ONES_ROWS, t), jnp.bfloat16)
    head_mask, bias_mask, bias_ones = [], [], []
    for j in range(2):
        h = 2 * pair + j
        head_mask.append((lane >= j * HEAD_DIM) & (lane < (j + 1) * HEAD_DIM))
        bias_mask.append((lane == BIAS_Q + h) | (lane == BIAS_Q + BIAS_MID + h)
                         | (lane == BIAS_Q + BIAS_LO + h))
        bias_ones.append(jnp.where((lane == h) | (lane == BIAS_MID + h) | (lane == BIAS_LO + h),
                                   1.0, 0.0).astype(jnp.bfloat16))

    for qi in range(n_q):
        q2 = q_ref[0, 0, qi * t:(qi + 1) * t, :]
        qb = qb_ref[0, qi * t:(qi + 1) * t, :]
        for j in range(2):
            qa_ref[qi, j] = jnp.concatenate(
                [jnp.where(head_mask[j], q2, jnp.zeros_like(q2)),
                 jnp.where(bias_mask[j], qb, bias_ones[j])], axis=1).T

    chains = [(kt, qi, j) for kt in range(n_q) for qi in range(kt, n_q) for j in range(2)]
    scores, probs, o_t = {}, {}, {}

    def score_stage(i):
        kt, qi, j = chains[i]
        ka = jnp.concatenate([k_ref[0, 0, kt * t:(kt + 1) * t, :],
                              kb_ref[0, kt * t:(kt + 1) * t, :]], axis=1)
        scores[i] = _dot(ka, qa_ref[qi, j])

    def softmax_stage(i):
        kt, qi, j = chains[i]
        s = scores.pop(i)
        if qi == kt:
            s = jnp.where(causal, s, -jnp.inf)
        m_new = jnp.max(s, axis=0, keepdims=True)
        alpha = None
        if kt > 0:
            m_old = m_ref[qi, j]
            m_new = jnp.maximum(m_old, m_new)
            alpha = jnp.exp2(m_old - m_new)
        if qi > kt:
            m_ref[qi, j] = m_new
        probs[i] = (jnp.exp2(s - m_new).astype(jnp.bfloat16), alpha)

    def value_stage(i):
        kt, qi, j = chains[i]
        p, alpha = probs.pop(i)
        v_aug = jnp.concatenate(
            [vt_ref[0, 0, kt, j * HEAD_DIM:(j + 1) * HEAD_DIM, :], ones_rows], axis=0)
        acc = _dot(v_aug, p)
        if kt > 0:
            acc = alpha * acc_ref[qi, j] + acc
        if qi > kt:
            acc_ref[qi, j] = acc
            return
        o_t[j] = acc[:HEAD_DIM] / acc[HEAD_DIM:HEAD_DIM + 1]
        if j == 1:
            o = jnp.concatenate([o_t.pop(0), o_t.pop(1)], axis=0).T
            gate = gate_ref[0, qi * t:(qi + 1) * t, :]
            o_ref[0, qi * t:(qi + 1) * t, :] = (jax.nn.sigmoid(gate) * o).astype(jnp.bfloat16)

    n = len(chains)
    for i in range(n + VALUE_LAG):
        if i < n:
            score_stage(i)
        if SOFTMAX_LAG <= i < n + SOFTMAX_LAG:
            softmax_stage(i - SOFTMAX_LAG)
        if i >= VALUE_LAG:
            value_stage(i - VALUE_LAG)


def _attention(q, q_bias, k, k_bias, v_t, gate):
    b, _, s, _ = q.shape
    n_q = s // ATT_T
    qk_spec =pl.BlockSpec((1, 1, s, LANES), lambda i, p: (i, p, 0, 0))
    bias_spec = pl.BlockSpec((1, s, LANES), lambda i, p: (i, 0, 0))
    return pl.pallas_call(
        _attn_kernel,
        name="fox_attention",
        grid=(b, N_PAIRS),
        in_specs=[
            qk_spec, bias_spec, qk_spec, bias_spec,
            pl.BlockSpec((1, 1, s // ATT_T, LANES, ATT_T), lambda i, p: (i, p, 0, 0, 0)),
            pl.BlockSpec((1, s, LANES), lambda i, p: (i, 0, p)),
        ],
        out_specs=pl.BlockSpec((1, s, LANES), lambda i, p: (i, 0, p)),
        out_shape=jax.ShapeDtypeStruct((b, s, D_MODEL), jnp.bfloat16),
        scratch_shapes=[
            pltpu.VMEM((n_q, 2, ATT_T, 2 * LANES), jnp.bfloat16),
            pltpu.VMEM((n_q, 2, 1, ATT_T), jnp.float32),
            pltpu.VMEM((n_q, 2, HEAD_DIM + ONES_ROWS, ATT_T), jnp.float32),
        ],
        compiler_params=pltpu.CompilerParams(
            dimension_semantics=("arbitrary", "arbitrary"), vmem_limit_bytes=VMEM_LIMIT),
    )(q, q_bias, k, k_bias, v_t, gate)


def _attn_out_kernel(x_ref, o_ref, w_ref, post_ref, out_ref):
    m = _dot(o_ref[...], w_ref[...])
    out_ref[...] = x_ref[...] + _rms(m, post_ref[...])


def _attn_out(x2d, o2d, w_o, post_g):
    t, d = x2d.shape
    row_spec = pl.BlockSpec((FFN_TM, d), lambda i: (i, 0))
    return pl.pallas_call(
        _attn_out_kernel,
        name="attn_out",
        grid=(t // FFN_TM,),
        in_specs=[row_spec, row_spec, _resident((d, d)), _resident((1, d))],
        out_specs=row_spec,
        out_shape=jax.ShapeDtypeStruct(x2d.shape, jnp.float32),
        compiler_params=pltpu.CompilerParams(
            dimension_semantics=("arbitrary",), vmem_limit_bytes=VMEM_LIMIT),
    )(x2d, o2d, w_o, post_g.reshape(1, -1))


def kernel(x, ffn1_pre_g, ffn1_post_g, ffn1_w_in, ffn1_w_out, mix_pre_g, mix_post_g,
           ffn2_pre_g, ffn2_post_g, ffn2_w_in, ffn2_w_out, conv_w_in, conv_k, conv_w_out,
           kv_g, kv_w, forget_b, attn_w_qg, attn_w_o):
    b, s, d = x.shape
    bf = jnp.bfloat16

    def ffn(h, pre, post, w_in, w_out):
        return _ffn(h.reshape(b * s, d), pre, post, w_in.astype(bf),
                    w_out.astype(bf)).reshape(b, s, d)

    h = ffn(x, ffn1_pre_g[0], ffn1_post_g[0], ffn1_w_in[0], ffn1_w_out[0])
    h = _conv_mixer(h, mix_pre_g[0], mix_post_g[0], conv_w_in[0].astype(bf), conv_k[0],
                    conv_w_out[0].astype(bf))
    h = ffn(h, ffn2_pre_g[0], ffn2_post_g[0], ffn2_w_in[0], ffn2_w_out[0])

    w_f = jnp.pad(kv_w[:, 2 * d:], ((0, 0), (0, LANES - N_HEADS))).astype(bf)
    fb = jnp.pad(forget_b, (0, LANES - N_HEADS)).reshape(1, LANES)
    k, v_t, k_bias, q_bias = _shared_kv(h, kv_g, kv_w[:, :2 * d].astype(bf), w_f, fb)

    h = ffn(h, ffn1_pre_g[1], ffn1_post_g[1], ffn1_w_in[1], ffn1_w_out[1])
    q, gate = _qg_proj(h, mix_pre_g[1], attn_w_qg[0].astype(bf))
    o = _attention(q, q_bias, k, k_bias, v_t, gate)
    h = _attn_out(h.reshape(b * s, d), o.reshape(b * s, d), attn_w_o[0].astype(bf),
                  mix_post_g[1]).reshape(b, s, d)
    h = ffn(h, ffn2_pre_g[1], ffn2_post_g[1], ffn2_w_in[1], ffn2_w_out[1])
    return h
```

```python
import math

import jax
import jax.numpy as jnp
from jax import lax
from jax.experimental import pallas as pl
from jax.experimental.pallas import tpu as pltpu

D_MODEL = 1024
N_HEADS = 16
HEAD_DIM = D_MODEL // N_HEADS
N_PAIRS = N_HEADS // 2
D_FF = 2816
RMS_EPS = 1e-6
LANES = 128
MXU_COLS = 256
VMEM_LIMIT = 56 * 1024 * 1024
CAST_BLOCK_BYTES = 4 * 1024 * 1024

FFN_TM = 1024
FFN_SUB = 512
FFN_FC = 256
MIX_TS = 512
MIX_CC = 256
KV_TS = 512
ATT_T = 256
ONES_ROWS = 16
LOG2_E = math.log2(math.e)
SOFTMAX_LAG = 3
VALUE_LAG = 6

BIAS_MID, BIAS_LO, BIAS_Q = 16, 32, 64


def _rms(x, g):
    return x * lax.rsqrt(jnp.mean(x * x, axis=-1, keepdims=True) + RMS_EPS) * g


def _dot(a, b):
    return jnp.dot(a, b, preferred_element_type=jnp.float32)


def _resident(shape):
    nd = len(shape)
    return pl.BlockSpec(shape, lambda *_: (0,) * nd, pipeline_mode=pl.Buffered(1))


def _resident_layer(shape, layer):
    return pl.BlockSpec((None,) + tuple(shape), lambda *_: (layer, 0, 0),
                        pipeline_mode=pl.Buffered(1))


def _cast_kernel(w_ref, o_ref):
    o_ref[...] = w_ref[...].astype(jnp.bfloat16)


def _to_bf16(w, cols=None):
    layers, rows, all_cols = w.shape
    cols = all_cols if cols is None else cols
    block_rows = rows
    while block_rows * cols * 4 > CAST_BLOCK_BYTES and block_rows % 32 == 0:
        block_rows //= 2
    spec = pl.BlockSpec((None, block_rows, cols), lambda l, i: (l, i, 0))
    return pl.pallas_call(
        _cast_kernel,
        name="to_bf16",
        grid=(layers, rows // block_rows),
        in_specs=[spec],
        out_specs=spec,
        out_shape=jax.ShapeDtypeStruct((layers, rows, cols), jnp.bfloat16),
        compiler_params=pltpu.CompilerParams(
            dimension_semantics=("arbitrary", "arbitrary"), vmem_limit_bytes=VMEM_LIMIT),
    )(w)


def _ffn_kernel(x_ref, pre_ref, post_ref, win_ref, wout_ref, o_ref):
    n_sub = x_ref.shape[0] // FFN_SUB
    n_chunks = D_FF // FFN_FC
    items = [(r, c) for r in range(n_sub) for c in range(n_chunks)]
    xn, acc, gate_up, hidden = {}, {}, {}, {}

    def rows(r):
        return slice(r * FFN_SUB, (r + 1) * FFN_SUB)

    def project(i):
        r, c = items[i]
        if c == 0:
            xn[r] = _rms(x_ref[rows(r), :], pre_ref[...]).astype(jnp.bfloat16)
            acc[r] = jnp.zeros((FFN_SUB, D_MODEL), jnp.float32)
        lo = c * FFN_FC
        gate_up[i] = (_dot(xn[r], win_ref[:, lo:lo + FFN_FC]),
                      _dot(xn[r], win_ref[:, D_FF + lo:D_FF + lo + FFN_FC]))

    def activate(i):
        gate, up = gate_up.pop(i)
        hidden[i] = (gate * jax.nn.sigmoid(gate) * up).astype(jnp.bfloat16)

    def output(i):
        r, c = items[i]
        lo = c * FFN_FC
        acc[r] = acc[r] + _dot(hidden.pop(i), wout_ref[lo:lo + FFN_FC, :])
        if c == n_chunks - 1:
            o_ref[rows(r), :] = x_ref[rows(r), :] + 0.5 * _rms(acc.pop(r), post_ref[...])

    n = len(items)
    for i in range(n + 2):
        if i < n:
            project(i)
        if 1 <= i <= n:
            activate(i - 1)
        if i >= 2:
            output(i - 2)


def _ffn(x2d, pre_g, post_g, w_in, w_out, layer):
    t = x2d.shape[0]
    return pl.pallas_call(
        _ffn_kernel,
        name="ffn",
        grid=(t // FFN_TM,),
        in_specs=[
            pl.BlockSpec((FFN_TM, D_MODEL), lambda i: (i, 0)),
            _resident((1, D_MODEL)),
            _resident((1, D_MODEL)),
            _resident_layer((D_MODEL, 2 * D_FF), layer),
            _resident_layer((D_FF, D_MODEL), layer),
        ],
        out_specs=pl.BlockSpec((FFN_TM, D_MODEL), lambda i: (i, 0)),
        out_shape=jax.ShapeDtypeStruct(x2d.shape, jnp.float32),
        compiler_params=pltpu.CompilerParams(
            dimension_semantics=("arbitrary",), vmem_limit_bytes=VMEM_LIMIT),
    )(x2d, pre_g.reshape(1, -1), post_g.reshape(1, -1), w_in, w_out)


def _mix_kernel(x_ref, pre_ref, post_ref, win_ref, ck_ref, wout_ref, o_ref, carry_ref):
    @pl.when(pl.program_id(1) == 0)
    def _():
        carry_ref[...] = jnp.zeros_like(carry_ref)

    x = x_ref[0]
    ts = x.shape[0]
    xn = _rms(x, pre_ref[...]).astype(jnp.bfloat16)
    row = lax.broadcasted_iota(jnp.int32, (ts, MIX_CC), 0)
    n_chunks = D_MODEL // MIX_CC
    proj, gated = {}, {}
    acc = [jnp.zeros(x.shape, jnp.float32)]

    def project(c):
        lo = c * MIX_CC
        proj[c] = tuple(_dot(xn, win_ref[:, part * D_MODEL + lo:part * D_MODEL + lo + MIX_CC])
                        for part in range(3))

    def conv(c):
        lo = c * MIX_CC
        b_gate, c_gate, h = proj.pop(c)
        u = c_gate * h
        prev1 = carry_ref[7:8, lo:lo + MIX_CC]
        prev2 = carry_ref[6:7, lo:lo + MIX_CC]
        u1 = jnp.where(row == 0, prev1, pltpu.roll(u, 1, axis=0))
        u2 = jnp.where(row == 0, prev2, jnp.where(row == 1, prev1, pltpu.roll(u, 2, axis=0)))
        carry_ref[:, lo:lo + MIX_CC] = u[ts - 8:, :]
        y = (ck_ref[0:1, lo:lo + MIX_CC] * u2 + ck_ref[1:2, lo:lo + MIX_CC] * u1
             + ck_ref[2:3, lo:lo + MIX_CC] * u)
        gated[c] = (b_gate * y).astype(jnp.bfloat16)

    def output(c):
        lo = c * MIX_CC
        acc[0] = acc[0] + _dot(gated.pop(c), wout_ref[lo:lo + MIX_CC, :])

    for step in range(n_chunks + 2):
        if step < n_chunks:
            project(step)
        if 1 <= step <= n_chunks:
            conv(step - 1)
        if step >= 2:
            output(step - 2)
    o_ref[0] = x + _rms(acc[0], post_ref[...])


def _conv_mixer(x, pre_g, post_g, w_in, conv_k, w_out):
    b, s, d = x.shape
    return pl.pallas_call(
        _mix_kernel,
        name="conv_mixer",
        grid=(b, s // MIX_TS),
        in_specs=[
            pl.BlockSpec((1, MIX_TS, d), lambda i, j: (i, j, 0)),
            _resident((1, d)),
            _resident((1, d)),
            _resident_layer((d, 3 * d), 0),
            _resident((3, d)),
            _resident_layer((d, d), 0),
        ],
        out_specs=pl.BlockSpec((1, MIX_TS, d), lambda i, j: (i, j, 0)),
        out_shape=jax.ShapeDtypeStruct(x.shape, jnp.float32),
        scratch_shapes=[pltpu.VMEM((8, d), jnp.float32)],
        compiler_params=pltpu.CompilerParams(
            dimension_semantics=("arbitrary", "arbitrary"), vmem_limit_bytes=VMEM_LIMIT),
    )(x, pre_g.reshape(1, -1), post_g.reshape(1, -1), w_in, conv_k, w_out)


def _split3(a):
    hi = a.astype(jnp.bfloat16)
    r = a - hi.astype(jnp.float32)
    mid = r.astype(jnp.bfloat16)
    lo = (r - mid.astype(jnp.float32)).astype(jnp.bfloat16)
    return hi, mid, lo


def _kv_kernel(x_ref, g_ref, wkv_ref, wf_ref, fb_ref, k_ref, vt_ref, kb_ref, qb_ref,
               run_ref, tri_ref):
    x = x_ref[0]
    ts = x.shape[0]

    @pl.when(pl.program_id(1) == 0)
    def _():
        run_ref[...] = jnp.zeros_like(run_ref)

    @pl.when((pl.program_id(0) == 0) & (pl.program_id(1) == 0))
    def _():
        r_idx = lax.broadcasted_iota(jnp.int32, (ts, ts), 0)
        c_idx = lax.broadcasted_iota(jnp.int32, (ts, ts), 1)
        tri_ref[...] = (c_idx <= r_idx).astype(jnp.bfloat16)

    xn = _rms(x, g_ref[...]).astype(jnp.bfloat16)
    lane = lax.broadcasted_iota(jnp.int32, (1, LANES), 1)
    is_head = lane < N_HEADS
    halves = MXU_COLS // LANES
    kv = {}

    def project(c):
        kv[c] = (_dot(xn, wkv_ref[:, c * MXU_COLS:(c + 1) * MXU_COLS]),
                 _dot(xn, wkv_ref[:, D_MODEL + c * MXU_COLS:D_MODEL + (c + 1) * MXU_COLS]))

    def store(c):
        k2, v2 = kv.pop(c)
        for half in range(halves):
            p = c * halves + half
            k_ref[0, p] = k2[:, half * LANES:(half + 1) * LANES].astype(jnp.bfloat16)
            for r in range(ts // ATT_T):
                vt_ref[0, p, r] = v2[r * ATT_T:(r + 1) * ATT_T,
                                     half * LANES:(half + 1) * LANES].T.astype(jnp.bfloat16)

    def log_forget():
        f_logit = _dot(xn, wf_ref[...]) + fb_ref[...]
        log_f = jnp.minimum(f_logit, 0.0) - jnp.log1p(jnp.exp(-jnp.abs(f_logit)))
        return jnp.where(is_head, log_f, 0.0)

    def cumulative(log_f):
        hi, mid, lo = (part.astype(jnp.float32) for part in _split3(log_f))
        packed = hi + pltpu.roll(mid, BIAS_MID, axis=1) + pltpu.roll(lo, BIAS_LO, axis=1)
        sums = _dot(tri_ref[...], packed.astype(jnp.bfloat16))
        sums = (sums + pltpu.roll(sums, LANES - BIAS_MID, axis=1)
                + pltpu.roll(sums, LANES - BIAS_LO, axis=1))
        csum = jnp.where(is_head, sums + run_ref[0:1, :], 0.0)
        run_ref[...] = jnp.broadcast_to(csum[ts - 1:ts, :], run_ref.shape)
        return csum

    def store_bias(csum):
        hi, mid, lo = (part.astype(jnp.float32) for part in _split3(csum * LOG2_E))
        key_bias = (jnp.where(lane >= BIAS_Q, 1.0, -hi) - pltpu.roll(mid, BIAS_MID, axis=1)
                    - pltpu.roll(lo, BIAS_LO, axis=1))
        query_bias = (pltpu.roll(hi, BIAS_Q, axis=1) + pltpu.roll(mid, BIAS_Q + BIAS_MID, axis=1)
                      + pltpu.roll(lo, BIAS_Q + BIAS_LO, axis=1))
        kb_ref[0] = key_bias.astype(jnp.bfloat16)
        qb_ref[0] = query_bias.astype(jnp.bfloat16)

    log_f = log_forget()
    project(0)
    csum = cumulative(log_f)
    project(1)
    store(0)
    store_bias(csum)
    for c in range(2, D_MODEL // MXU_COLS):
        project(c)
        store(c - 1)
    store(D_MODEL // MXU_COLS - 1)


def _shared_kv(x, kv_g, w_kv, w_f, fb):
    b, s, d = x.shape
    bias_shape = jax.ShapeDtypeStruct((b, s, LANES), jnp.bfloat16)
    bias_spec = pl.BlockSpec((1, KV_TS, LANES), lambda i, j: (i, j, 0))
    n_sub = KV_TS // ATT_T
    return pl.pallas_call(
        _kv_kernel,
        name="shared_kv",
        grid=(b, s // KV_TS),
        in_specs=[
            pl.BlockSpec((1, KV_TS, d), lambda i, j: (i, j, 0)),
            _resident((1, d)),
            _resident_layer((d, 2 * d), 0),
            _resident((d, LANES)),
            _resident((1, LANES)),
        ],
        out_specs=[
            pl.BlockSpec((1, N_PAIRS, KV_TS, LANES), lambda i, j: (i, 0, j, 0)),
            pl.BlockSpec((1, N_PAIRS, n_sub, LANES, ATT_T), lambda i, j: (i, 0, j, 0, 0)),
            bias_spec, bias_spec],
        out_shape=[
            jax.ShapeDtypeStruct((b, N_PAIRS, s, LANES), jnp.bfloat16),
            jax.ShapeDtypeStruct((b, N_PAIRS, s // ATT_T, LANES, ATT_T), jnp.bfloat16),
            bias_shape, bias_shape],
        scratch_shapes=[pltpu.VMEM((8, LANES), jnp.float32),
                        pltpu.VMEM((KV_TS, KV_TS), jnp.bfloat16)],
        compiler_params=pltpu.CompilerParams(
            dimension_semantics=("arbitrary", "arbitrary"), vmem_limit_bytes=VMEM_LIMIT),
    )(x, kv_g.reshape(1, -1), w_kv, w_f, fb)


def _qg_kernel(x_ref, g_ref, w_ref, q_ref, gate_ref):
    x = x_ref[0]
    xn = _rms(x, g_ref[...]).astype(jnp.bfloat16)
    scale = LOG2_E / math.sqrt(HEAD_DIM)
    for c in range(D_MODEL // MXU_COLS):
        q2 = _dot(xn, w_ref[:, c * MXU_COLS:(c + 1) * MXU_COLS]) * scale
        for half in range(MXU_COLS // LANES):
            q_ref[0, c * (MXU_COLS // LANES) + half] = (
                q2[:, half * LANES:(half + 1) * LANES].astype(jnp.bfloat16))
    gate_ref[0] = _dot(xn, w_ref[:, D_MODEL:])


def _qg_proj(x, g, w_qg):
    b, s, d = x.shape
    return pl.pallas_call(
        _qg_kernel,
        name="qg_proj",
        grid=(b, s // KV_TS),
        in_specs=[
            pl.BlockSpec((1, KV_TS, d), lambda i, j: (i, j, 0)),
            _resident((1, d)),
            _resident_layer((d, 2 * d), 0),
        ],
        out_specs=[pl.BlockSpec((1, N_PAIRS, KV_TS, LANES), lambda i, j: (i, 0, j, 0)),
                   pl.BlockSpec((1, KV_TS, d), lambda i, j: (i, j, 0))],
        out_shape=[jax.ShapeDtypeStruct((b, N_PAIRS, s, LANES), jnp.bfloat16),
                   jax.ShapeDtypeStruct((b, s, d), jnp.float32)],
        compiler_params=pltpu.CompilerParams(
            dimension_semantics=("arbitrary", "arbitrary"), vmem_limit_bytes=VMEM_LIMIT),
    )(x, g.reshape(1, -1), w_qg)


def _attn_kernel(q_ref, qb_ref, k_ref, kb_ref, vt_ref, gate_ref, o_ref, qa_ref, m_ref, acc_ref):
    pair = pl.program_id(1)
    t = ATT_T
    n_q = q_ref.shape[2] // t
    lane = lax.broadcasted_iota(jnp.int32, (1, LANES), 1)
    key_idx = lax.broadcasted_iota(jnp.int32, (t, t), 0)
    query_idx = lax.broadcasted_iota(jnp.int32, (t, t), 1)
    causal = key_idx <= query_idx
    ones_rows = jnp.ones((ONES_ROWS, t), jnp.bfloat16)
    head_mask, bias_mask, bias_ones = [], [], []
    for j in range(2):
        h = 2 * pair + j
        head_mask.append((lane >= j * HEAD_DIM) & (lane < (j + 1) * HEAD_DIM))
        bias_mask.append((lane == BIAS_Q + h) | (lane == BIAS_Q + BIAS_MID + h)
                         | (lane == BIAS_Q + BIAS_LO + h))
        bias_ones.append(jnp.where((lane == h) | (lane == BIAS_MID + h) | (lane == BIAS_LO + h),
                                   1.0, 0.0).astype(jnp.bfloat16))

    for qi in range(n_q):
        q2 = q_ref[0, 0, qi * t:(qi + 1) * t, :]
        qb = qb_ref[0, qi * t:(qi + 1) * t, :]
        for j in range(2):
            qa_ref[qi, j] = jnp.concatenate(
                [jnp.where(head_mask[j], q2, jnp.zeros_like(q2)),
                 jnp.where(bias_mask[j], qb, bias_ones[j])], axis=1).T

    chains = [(kt, qi, j) for kt in range(n_q) for qi in range(kt, n_q) for j in range(2)]
    scores, probs, o_t = {}, {}, {}

    def score_stage(i):
        kt, qi, j = chains[i]
        ka = jnp.concatenate([k_ref[0, 0, kt * t:(kt + 1) * t, :],
                              kb_ref[0, kt * t:(kt + 1) * t, :]], axis=1)
        scores[i] = _dot(ka, qa_ref[qi, j])

    def softmax_stage(i):
        kt, qi, j = chains[i]
        s = scores.pop(i)
        if qi == kt:
            s = jnp.where(causal, s, -jnp.inf)
        m_new = jnp.max(s, axis=0, keepdims=True)
        alpha = None
        if kt > 0:
            m_old = m_ref[qi, j]
            m_new = jnp.maximum(m_old, m_new)
            alpha = jnp.exp2(m_old - m_new)
        if qi > kt:
            m_ref[qi, j] = m_new
        probs[i] = (jnp.exp2(s - m_new).astype(jnp.bfloat16), alpha)

    def value_stage(i):
        kt, qi, j = chains[i]
        p, alpha = probs.pop(i)
        v_aug = jnp.concatenate(
            [vt_ref[0, 0, kt, j * HEAD_DIM:(j + 1) * HEAD_DIM, :], ones_rows], axis=0)
        acc = _dot(v_aug, p)
        if kt > 0:
            acc = alpha * acc_ref[qi, j] + acc
        if qi > kt:
            acc_ref[qi, j] = acc
            return
        o_t[j] = acc[:HEAD_DIM] / acc[HEAD_DIM:HEAD_DIM + 1]
        if j == 1:
            o = jnp.concatenate([o_t.pop(0), o_t.pop(1)], axis=0).T
            gate = gate_ref[0, qi * t:(qi + 1) * t, :]
            o_ref[0, qi * t:(qi + 1) * t, :] = (jax.nn.sigmoid(gate) * o).astype(jnp.bfloat16)

    n = len(chains)
    for i in range(n + VALUE_LAG):
        if i < n:
            score_stage(i)
        if SOFTMAX_LAG <= i < n + SOFTMAX_LAG:
            softmax_stage(i - SOFTMAX_LAG)
        if i >= VALUE_LAG:
            value_stage(i - VALUE_LAG)


def _attention(q, q_bias, k, k_bias, v_t, gate):
    b, _, s, _ = q.shape
    n_q = s // ATT_T
    qk_spec = pl.BlockSpec((1, 1, s, LANES), lambda i, p: (i, p, 0, 0))
    bias_spec = pl.BlockSpec((1, s, LANES), lambda i, p: (i, 0, 0))
    return pl.pallas_call(
        _attn_kernel,
        name="fox_attention",
        grid=(b, N_PAIRS),
        in_specs=[
            qk_spec, bias_spec, qk_spec, bias_spec,
            pl.BlockSpec((1, 1, s // ATT_T, LANES, ATT_T), lambda i, p: (i, p, 0, 0, 0)),
            pl.BlockSpec((1, s, LANES), lambda i, p: (i, 0, p)),
        ],
        out_specs=pl.BlockSpec((1, s, LANES), lambda i, p: (i, 0, p)),
        out_shape=jax.ShapeDtypeStruct((b, s, D_MODEL), jnp.bfloat16),
        scratch_shapes=[
            pltpu.VMEM((n_q, 2, ATT_T, 2 * LANES), jnp.bfloat16),
            pltpu.VMEM((n_q, 2, 1, ATT_T), jnp.float32),
            pltpu.VMEM((n_q, 2, HEAD_DIM + ONES_ROWS, ATT_T), jnp.float32),
        ],
        compiler_params=pltpu.CompilerParams(
            dimension_semantics=("arbitrary", "arbitrary"), vmem_limit_bytes=VMEM_LIMIT),
    )(q, q_bias, k, k_bias, v_t, gate)


def _attn_out_kernel(x_ref, o_ref, w_ref, post_ref, out_ref):
    m = _dot(o_ref[...], w_ref[...])
    out_ref[...] = x_ref[...] + _rms(m, post_ref[...])


def _attn_out(x2d, o2d, w_o, post_g):
    t, d = x2d.shape
    row_spec = pl.BlockSpec((FFN_TM, d), lambda i: (i, 0))
    return pl.pallas_call(
        _attn_out_kernel,
        name="attn_out",
        grid=(t // FFN_TM,),
        in_specs=[row_spec, row_spec, _resident_layer((d, d), 0), _resident((1, d))],
        out_specs=row_spec,
        out_shape=jax.ShapeDtypeStruct(x2d.shape, jnp.float32),
        compiler_params=pltpu.CompilerParams(
            dimension_semantics=("arbitrary",), vmem_limit_bytes=VMEM_LIMIT),
    )(x2d, o2d, w_o, post_g.reshape(1, -1))


def kernel(x, ffn1_pre_g, ffn1_post_g, ffn1_w_in, ffn1_w_out, mix_pre_g, mix_post_g,
           ffn2_pre_g, ffn2_post_g, ffn2_w_in, ffn2_w_out, conv_w_in, conv_k, conv_w_out,
           kv_g, kv_w, forget_b, attn_w_qg, attn_w_o):
    b, s, d = x.shape
    w1_in, w1_out = _to_bf16(ffn1_w_in), _to_bf16(ffn1_w_out)
    w2_in, w2_out = _to_bf16(ffn2_w_in), _to_bf16(ffn2_w_out)

    def ffn(h, pre, post, w_in, w_out, layer):
        return _ffn(h.reshape(b * s, d), pre[layer], post[layer], w_in, w_out,
                    layer).reshape(b, s, d)

    h = ffn(x, ffn1_pre_g, ffn1_post_g, w1_in, w1_out, 0)
    h = _conv_mixer(h, mix_pre_g[0], mix_post_g[0], _to_bf16(conv_w_in), conv_k[0],
                    _to_bf16(conv_w_out))
    h = ffn(h, ffn2_pre_g, ffn2_post_g, w2_in, w2_out, 0)

    w_f = jnp.pad(kv_w[:, 2 * d:], ((0, 0), (0, LANES - N_HEADS))).astype(jnp.bfloat16)
    fb = jnp.pad(forget_b, (0, LANES - N_HEADS)).reshape(1, LANES)
    k, v_t, k_bias, q_bias = _shared_kv(h, kv_g, _to_bf16(kv_w[None], cols=2 * d), w_f, fb)

    h = ffn(h, ffn1_pre_g, ffn1_post_g, w1_in, w1_out, 1)
    q, gate = _qg_proj(h, mix_pre_g[1], _to_bf16(attn_w_qg))
    o = _attention(q, q_bias, k, k_bias, v_t, gate)
    h = _attn_out(h.reshape(b * s, d), o.reshape(b * s, d), _to_bf16(attn_w_o),
                  mix_post_g[1]).reshape(b, s, d)
    h = ffn(h, ffn2_pre_g, ffn2_post_g, w2_in, w2_out, 1)
    return h
```

```python
import math

import jax
import jax.numpy as jnp
from jax import lax
from jax.experimental import pallas as pl
from jax.experimental.pallas import tpu as pltpu

D_MODEL = 1024
N_HEADS = 16
HEAD_DIM = D_MODEL // N_HEADS
N_PAIRS = N_HEADS // 2
D_FF = 2816
RMS_EPS = 1e-6
LANES = 128
MXU_COLS = 256
VMEM_LIMIT = 56 * 1024 * 1024
CAST_BLOCK_BYTES = 4 * 1024 * 1024

FFN_TM = 1024
FFN_SUB = 512
FFN_FC = 256
MIX_TS = 512
MIX_CC = 256
ATT_T = 256
ONES_ROWS = 16
LOG2_E = math.log2(math.e)
SOFTMAX_LAG = 3
VALUE_LAG = 6

BIAS_MID, BIAS_LO, BIAS_Q = 16, 32, 64


def _rms(x, g):
    return x * lax.rsqrt(jnp.mean(x * x, axis=-1, keepdims=True) + RMS_EPS) * g


def _dot(a, b):
    return jnp.dot(a, b, preferred_element_type=jnp.float32)


def _resident(shape):
    nd = len(shape)
    return pl.BlockSpec(shape, lambda *_: (0,) * nd, pipeline_mode=pl.Buffered(1))


def _resident_layer(shape, layer):
    return pl.BlockSpec((None,) + tuple(shape), lambda *_: (layer, 0, 0),
                        pipeline_mode=pl.Buffered(1))


def _cast_kernel(w_ref, o_ref):
    o_ref[...] = w_ref[...].astype(jnp.bfloat16)


def _to_bf16(w, cols=None):
    layers, rows, all_cols = w.shape
    cols = all_cols if cols is None else cols
    block_rows = rows
    while block_rows * cols * 4 > CAST_BLOCK_BYTES and block_rows % 32 == 0:
        block_rows //= 2
    spec = pl.BlockSpec((None, block_rows, cols), lambda l, i: (l, i, 0))
    return pl.pallas_call(
        _cast_kernel,
        name="to_bf16",
        grid=(layers, rows // block_rows),
        in_specs=[spec],
        out_specs=spec,
        out_shape=jax.ShapeDtypeStruct((layers, rows, cols), jnp.bfloat16),
        compiler_params=pltpu.CompilerParams(
            dimension_semantics=("arbitrary", "arbitrary"), vmem_limit_bytes=VMEM_LIMIT),
    )(w)


def _split3(a):
    hi = a.astype(jnp.bfloat16)
    r = a - hi.astype(jnp.float32)
    mid = r.astype(jnp.bfloat16)
    lo = (r - mid.astype(jnp.float32)).astype(jnp.bfloat16)
    return hi, mid, lo


def _make_ffn_kernel(n_sub, attn_out, kv, qg, seq_tiles):
    n_chunks = D_FF // FFN_FC
    groups = D_MODEL // MXU_COLS
    halves = MXU_COLS // LANES
    assert n_sub == 1 or not (kv or qg)

    def body(*refs):
        refs = list(refs)

        def take(n):
            return [refs.pop(0) for _ in range(n)]

        (x_ref,) = take(1)
        if attn_out:
            og_ref, wo_ref, mixpost_ref = take(3)
        pre_ref, post_ref, win_ref, wout_ref = take(4)
        if kv:
            kvg_ref, wkv_ref, wf_ref, fb_ref = take(4)
        if qg:
            mixpre_ref, wqg_ref = take(2)
        (o_ref,) = take(1)
        if kv:
            k_ref, vt_ref, kb_ref, qb_ref = take(4)
        if qg:
            q_ref, gate_ref = take(2)
        if kv:
            run_ref, tri_ref = take(2)

            @pl.when(pl.program_id(0) % seq_tiles == 0)
            def _():
                run_ref[...] = jnp.zeros_like(run_ref)

            @pl.when(pl.program_id(0) == 0)
            def _():
                r_idx = lax.broadcasted_iota(jnp.int32, tri_ref.shape, 0)
                c_idx = lax.broadcasted_iota(jnp.int32, tri_ref.shape, 1)
                tri_ref[...] = (c_idx <= r_idx).astype(jnp.bfloat16)

        x_in, xn, x_kv, x_q, acc, stash = {}, {}, {}, {}, {}, {}

        def rows(r):
            return slice(r * FFN_SUB, (r + 1) * FFN_SUB)

        def normalize(r, x):
            x_in[r] = x
            xhat = x * lax.rsqrt(jnp.mean(x * x, axis=-1, keepdims=True) + RMS_EPS)
            xn[r] = (xhat * pre_ref[...]).astype(jnp.bfloat16)
            if kv:
                x_kv[r] = (xhat * kvg_ref[...]).astype(jnp.bfloat16)
            acc[r] = jnp.zeros((FFN_SUB, D_MODEL), jnp.float32)

        def attn_out_items(r):
            parts = []

            def make(c):
                def project():
                    parts.append(_dot(og_ref[rows(r), :],
                                      wo_ref[:, c * MXU_COLS:(c + 1) * MXU_COLS]))

                def finish():
                    if c == groups - 1:
                        mixed = jnp.concatenate(parts, axis=1)
                        normalize(r, x_ref[rows(r), :] + _rms(mixed, mixpost_ref[...]))
                return project, None, finish
            return [make(c) for c in range(groups)]

        def ffn_items(r):
            def make(c):
                lo = c * FFN_FC
                key = ("ffn", r, c)

                def project():
                    if c == 0 and not attn_out:
                        normalize(r, x_ref[rows(r), :])
                    stash[key] = (_dot(xn[r], win_ref[:, lo:lo + FFN_FC]),
                                  _dot(xn[r], win_ref[:, D_FF + lo:D_FF + lo + FFN_FC]))

                def activate():
                    gate, up = stash.pop(key)
                    stash[key] = (gate * jax.nn.sigmoid(gate) * up).astype(jnp.bfloat16)

                def output():
                    acc[r] = acc[r] + _dot(stash.pop(key), wout_ref[lo:lo + FFN_FC, :])
                    if c == n_chunks - 1:
                        out = x_in.pop(r) + 0.5 * _rms(acc.pop(r), post_ref[...])
                        o_ref[rows(r), :] = out
                        if qg:
                            x_q[r] = _rms(out, mixpre_ref[...]).astype(jnp.bfloat16)
                return project, activate, output
            return [make(c) for c in range(n_chunks)]

        def kv_items(r):
            ts = FFN_SUB
            lane = lax.broadcasted_iota(jnp.int32, (1, LANES), 1)
            is_head = lane < N_HEADS

            def logit():
                stash["f"] = _dot(x_kv[r], wf_ref[...]) + fb_ref[...]

            def log_forget():
                f_logit = stash.pop("f")
                log_f = jnp.minimum(f_logit, 0.0) - jnp.log1p(jnp.exp(-jnp.abs(f_logit)))
                hi, mid, lo = (part.astype(jnp.float32)
                               for part in _split3(jnp.where(is_head, log_f, 0.0)))
                packed = hi + pltpu.roll(mid, BIAS_MID, axis=1) + pltpu.roll(lo, BIAS_LO, axis=1)
                stash["f"] = packed.astype(jnp.bfloat16)

            def cumulative():
                sums = _dot(tri_ref[...], stash.pop("f"))
                sums = (sums + pltpu.roll(sums, LANES - BIAS_MID, axis=1)
                        + pltpu.roll(sums, LANES - BIAS_LO, axis=1))
                csum = jnp.where(is_head, sums + run_ref[0:1, :], 0.0)
                run_ref[...] = jnp.broadcast_to(csum[ts - 1:ts, :], run_ref.shape)
                hi, mid, lo = (part.astype(jnp.float32) for part in _split3(csum * LOG2_E))
                key_bias = (jnp.where(lane >= BIAS_Q, 1.0, -hi)
                            - pltpu.roll(mid, BIAS_MID, axis=1) - pltpu.roll(lo, BIAS_LO, axis=1))
                query_bias = (pltpu.roll(hi, BIAS_Q, axis=1)
                              + pltpu.roll(mid, BIAS_Q + BIAS_MID, axis=1)
                              + pltpu.roll(lo, BIAS_Q + BIAS_LO, axis=1))
                kb_ref[0] = key_bias.astype(jnp.bfloat16)
                qb_ref[0] = query_bias.astype(jnp.bfloat16)

            def make(c):
                key = ("kv", c)

                def project():
                    stash[key] = (
                        _dot(x_kv[r], wkv_ref[:, c * MXU_COLS:(c + 1) * MXU_COLS]),
                        _dot(x_kv[r],
                             wkv_ref[:, D_MODEL + c * MXU_COLS:D_MODEL + (c + 1) * MXU_COLS]))

                def store():
                    k2, v2 = stash.pop(key)
                    for half in range(halves):
                        p = c * halves + half
                        k_ref[0, p] = k2[:, half * LANES:(half + 1) * LANES].astype(jnp.bfloat16)
                        for blk in range(ts // ATT_T):
                            vt_ref[0, p, blk] = v2[
                                blk * ATT_T:(blk + 1) * ATT_T,
                                half * LANES:(half + 1) * LANES].T.astype(jnp.bfloat16)
                return project, None, store
            return [(logit, log_forget, cumulative)] + [make(c) for c in range(groups)]

        def qg_items(r):
            scale = LOG2_E / math.sqrt(HEAD_DIM)

            def make(c):
                key = ("qg", c)

                def project():
                    stash[key] = _dot(x_q[r], wqg_ref[:, c * MXU_COLS:(c + 1) * MXU_COLS])

                def store():
                    val = stash.pop(key)
                    if c >= groups:
                        gate_ref[:, (c - groups) * MXU_COLS:(c - groups + 1) * MXU_COLS] = val
                        return
                    val = val * scale
                    for half in range(halves):
                        q_ref[0, c * halves + half] = (
                            val[:, half * LANES:(half + 1) * LANES].astype(jnp.bfloat16))
                return project, None, store
            return [make(c) for c in range(2 * groups)]

        bubble = (None, None, None)
        items = []
        if attn_out:
            items += attn_out_items(0) + [bubble, bubble]
        for r in range(n_sub):
            ffn = ffn_items(r)
            if attn_out and r + 1 < n_sub:
                ffn = ffn[:n_chunks // 2] + attn_out_items(r + 1) + ffn[n_chunks // 2:]
            items += ffn
        if kv:
            items += kv_items(0)
        if qg:
            items += ([] if kv else [bubble, bubble]) + qg_items(0)

        for i in range(len(items) + 2):
            for lag in range(3):
                if 0 <= i - lag < len(items) and items[i - lag][lag] is not None:
                    items[i - lag][lag]()

    return body


def _ffn(x2d, pre_g, post_g, w_in, w_out, layer, *, tm=FFN_TM, attn_out=None, kv=None, qg=None,
         seq_len=None):
    t, d = x2d.shape
    row_spec = pl.BlockSpec((tm, d), lambda i: (i, 0))
    vec_spec = _resident((1, d))
    operands, in_specs = [x2d], [row_spec]
    if attn_out is not None:
        og, w_o, mix_post_g = attn_out
        operands += [og, w_o, mix_post_g.reshape(1, -1)]
        in_specs += [row_spec, _resident_layer((d, d), 0), vec_spec]
    operands += [pre_g.reshape(1, -1), post_g.reshape(1, -1), w_in, w_out]
    in_specs += [vec_spec, vec_spec, _resident_layer((d, 2 * D_FF), layer),
                 _resident_layer((D_FF, d), layer)]
    out_shape = [jax.ShapeDtypeStruct((t, d), jnp.float32)]
    out_specs = [row_spec]
    scratch = []
    seq_tiles = None
    if kv is not None or qg is not None:
        seq_tiles = seq_len // tm
        bsz = t // seq_len
        pair_shape = jax.ShapeDtypeStruct((bsz, N_PAIRS, seq_len, LANES), jnp.bfloat16)
        pair_spec = pl.BlockSpec((1, N_PAIRS, tm, LANES),
                                 lambda i: (i // seq_tiles, 0, i % seq_tiles, 0))
    if kv is not None:
        kv_g, w_kv, w_f, fb = kv
        operands += [kv_g.reshape(1, -1), w_kv, w_f, fb]
        in_specs += [vec_spec, _resident_layer((d, 2 * d), 0), _resident((d, LANES)),
                     _resident((1, LANES))]
        bias_shape = jax.ShapeDtypeStruct((bsz, seq_len, LANES), jnp.bfloat16)
        bias_spec = pl.BlockSpec((1, tm, LANES), lambda i: (i // seq_tiles, i % seq_tiles, 0))
        out_shape += [
            pair_shape,
            jax.ShapeDtypeStruct((bsz, N_PAIRS, seq_len // ATT_T, LANES, ATT_T), jnp.bfloat16),
            bias_shape, bias_shape]
        out_specs += [
            pair_spec,
            pl.BlockSpec((1, N_PAIRS, tm // ATT_T, LANES, ATT_T),
                         lambda i: (i // seq_tiles, 0, i % seq_tiles, 0, 0)),
            bias_spec, bias_spec]
        scratch += [pltpu.VMEM((8, LANES), jnp.float32),
                    pltpu.VMEM((tm, tm), jnp.bfloat16)]
    if qg is not None:
        mix_pre_g, w_qg = qg
        operands += [mix_pre_g.reshape(1, -1), w_qg]
        in_specs += [vec_spec, _resident_layer((d, 2 * d), 0)]
        out_shape += [pair_shape, jax.ShapeDtypeStruct((t, d), jnp.float32)]
        out_specs += [pair_spec, row_spec]
    return pl.pallas_call(
        _make_ffn_kernel(tm // FFN_SUB, attn_out is not None, kv is not None, qg is not None,
                         seq_tiles),
        name="ffn",
        grid=(t // tm,),
        in_specs=in_specs,
        out_specs=out_specs,
        out_shape=out_shape,
        scratch_shapes=scratch,
        compiler_params=pltpu.CompilerParams(
            dimension_semantics=("arbitrary",), vmem_limit_bytes=VMEM_LIMIT),
    )(*operands)


def _mix_kernel(x_ref, pre_ref, post_ref, win_ref, ck_ref, wout_ref, o_ref, carry_ref):
    @pl.when(pl.program_id(1) == 0)
    def _():
        carry_ref[...] = jnp.zeros_like(carry_ref)

    x = x_ref[0]
    ts = x.shape[0]
    xn = _rms(x, pre_ref[...]).astype(jnp.bfloat16)
    row = lax.broadcasted_iota(jnp.int32, (ts, MIX_CC), 0)
    n_chunks = D_MODEL // MIX_CC
    proj, gated = {}, {}
    acc = [jnp.zeros(x.shape, jnp.float32)]

    def project(c):
        lo = c * MIX_CC
        proj[c] = tuple(_dot(xn, win_ref[:, part * D_MODEL + lo:part * D_MODEL + lo + MIX_CC])
                        for part in range(3))

    def conv(c):
        lo = c * MIX_CC
        b_gate, c_gate, h = proj.pop(c)
        u = c_gate * h
        prev1 = carry_ref[7:8, lo:lo + MIX_CC]
        prev2 = carry_ref[6:7, lo:lo + MIX_CC]
        u1 = jnp.where(row == 0, prev1, pltpu.roll(u, 1, axis=0))
        u2 = jnp.where(row == 0, prev2, jnp.where(row == 1, prev1, pltpu.roll(u, 2, axis=0)))
        carry_ref[:, lo:lo + MIX_CC] = u[ts - 8:, :]
        y = (ck_ref[0:1, lo:lo + MIX_CC] * u2 + ck_ref[1:2, lo:lo + MIX_CC] * u1
             + ck_ref[2:3, lo:lo + MIX_CC] * u)
        gated[c] = (b_gate * y).astype(jnp.bfloat16)

    def output(c):
        lo = c * MIX_CC
        acc[0] = acc[0] + _dot(gated.pop(c), wout_ref[lo:lo + MIX_CC, :])

    for step in range(n_chunks + 2):
        if step < n_chunks:
            project(step)
        if 1 <= step <= n_chunks:
            conv(step - 1)
        if step >= 2:
            output(step - 2)
    o_ref[0] = x + _rms(acc[0], post_ref[...])


def _conv_mixer(x, pre_g, post_g, w_in, conv_k, w_out):
    b, s, d = x.shape
    return pl.pallas_call(
        _mix_kernel,
        name="conv_mixer",
        grid=(b, s // MIX_TS),
        in_specs=[
            pl.BlockSpec((1, MIX_TS, d), lambda i, j: (i, j, 0)),
            _resident((1, d)),
            _resident((1, d)),
            _resident_layer((d, 3 * d), 0),
            _resident((3, d)),
            _resident_layer((d, d), 0),
        ],
        out_specs=pl.BlockSpec((1, MIX_TS, d), lambda i, j: (i, j, 0)),
        out_shape=jax.ShapeDtypeStruct(x.shape, jnp.float32),
        scratch_shapes=[pltpu.VMEM((8, d), jnp.float32)],
        compiler_params=pltpu.CompilerParams(
            dimension_semantics=("arbitrary", "arbitrary"), vmem_limit_bytes=VMEM_LIMIT),
    )(x, pre_g.reshape(1, -1), post_g.reshape(1, -1), w_in, conv_k, w_out)


def _attn_kernel(q_ref, qb_ref, k_ref, kb_ref, vt_ref, gate_ref, o_ref, qa_ref, m_ref, acc_ref):
    pair = pl.program_id(1)
    t = ATT_T
    n_q = q_ref.shape[2] // t
    lane = lax.broadcasted_iota(jnp.int32, (1, LANES), 1)
    key_idx = lax.broadcasted_iota(jnp.int32, (t, t), 0)
    query_idx = lax.broadcasted_iota(jnp.int32, (t, t), 1)
    causal = key_idx <= query_idx
    ones_rows = jnp.ones((ONES_ROWS, t), jnp.bfloat16)
    head_mask, bias_mask, bias_ones = [], [], []
    for j in range(2):
        h = 2 * pair + j
        head_mask.append((lane >= j * HEAD_DIM) & (lane < (j + 1) * HEAD_DIM))
        bias_mask.append((lane == BIAS_Q + h) | (lane == BIAS_Q + BIAS_MID + h)
                         | (lane == BIAS_Q + BIAS_LO + h))
        bias_ones.append(jnp.where((lane == h) | (lane == BIAS_MID + h) | (lane == BIAS_LO + h),
                                   1.0, 0.0).astype(jnp.bfloat16))

    for qi in range(n_q):
        q2 = q_ref[0, 0, qi * t:(qi + 1) * t, :]
        qb = qb_ref[0, qi * t:(qi + 1) * t, :]
        for j in range(2):
            qa_ref[qi, j] = jnp.concatenate(
                [jnp.where(head_mask[j], q2, jnp.zeros_like(q2)),
                 jnp.where(bias_mask[j], qb, bias_ones[j])], axis=1).T

    chains = [(kt, qi, j) for kt in range(n_q) for qi in range(kt, n_q) for j in range(2)]
    scores, probs, o_t = {}, {}, {}

    def score_stage(i):
        kt, qi, j = chains[i]
        ka = jnp.concatenate([k_ref[0, 0, kt * t:(kt + 1) * t, :],
                              kb_ref[0, kt * t:(kt + 1) * t, :]], axis=1)
        scores[i] = _dot(ka, qa_ref[qi, j])

    def softmax_stage(i):
        kt, qi, j = chains[i]
        s = scores.pop(i)
        if qi == kt:
            s = jnp.where(causal, s, -jnp.inf)
        m_new = jnp.max(s, axis=0, keepdims=True)
        alpha = None
        if kt > 0:
            m_old = m_ref[qi, j]
            m_new = jnp.maximum(m_old, m_new)
            alpha = jnp.exp2(m_old - m_new)
        if qi > kt:
            m_ref[qi, j] = m_new
        probs[i] = (jnp.exp2(s - m_new).astype(jnp.bfloat16), alpha)

    def value_stage(i):
        kt, qi, j = chains[i]
        p, alpha = probs.pop(i)
        v_aug = jnp.concatenate(
            [vt_ref[0, 0, kt, j * HEAD_DIM:(j + 1) * HEAD_DIM, :], ones_rows], axis=0)
        acc = _dot(v_aug, p)
        if kt > 0:
            acc = alpha * acc_ref[qi, j] + acc
        if qi > kt:
            acc_ref[qi, j] = acc
            return
        o_t[j] = acc[:HEAD_DIM] / acc[HEAD_DIM:HEAD_DIM + 1]
        if j == 1:
            o = jnp.concatenate([o_t.pop(0), o_t.pop(1)], axis=0).T
            gate = gate_ref[0, qi * t:(qi + 1) * t, :]
            o_ref[0, qi * t:(qi + 1) * t, :] = (jax.nn.sigmoid(gate) * o).astype(jnp.bfloat16)

    n = len(chains)
    for i in range(n + VALUE_LAG):
        if i < n:
            score_stage(i)
        if SOFTMAX_LAG <= i < n + SOFTMAX_LAG:
            softmax_stage(i - SOFTMAX_LAG)
        if i >= VALUE_LAG:
            value_stage(i - VALUE_LAG)


def _attention(q, q_bias, k, k_bias, v_t, gate):
    b, _, s, _ = q.shape
    n_q = s // ATT_T
    qk_spec = pl.BlockSpec((1, 1, s, LANES), lambda i, p: (i, p, 0, 0))
    bias_spec = pl.BlockSpec((1, s, LANES), lambda i, p: (i, 0, 0))
    return pl.pallas_call(
        _attn_kernel,
        name="fox_attention",
        grid=(b, N_PAIRS),
        in_specs=[
            qk_spec, bias_spec, qk_spec, bias_spec,
            pl.BlockSpec((1, 1, s // ATT_T, LANES, ATT_T), lambda i, p: (i, p, 0, 0, 0)),
            pl.BlockSpec((1, s, LANES), lambda i, p: (i, 0, p)),
        ],
        out_specs=pl.BlockSpec((1, s, LANES), lambda i, p: (i, 0, p)),
        out_shape=jax.ShapeDtypeStruct((b, s, D_MODEL), jnp.bfloat16),
        scratch_shapes=[
            pltpu.VMEM((n_q, 2, ATT_T, 2 * LANES), jnp.bfloat16),
            pltpu.VMEM((n_q, 2, 1, ATT_T), jnp.float32),
            pltpu.VMEM((n_q, 2, HEAD_DIM + ONES_ROWS, ATT_T), jnp.float32),
        ],
        compiler_params=pltpu.CompilerParams(
            dimension_semantics=("arbitrary", "arbitrary"), vmem_limit_bytes=VMEM_LIMIT),
    )(q, q_bias, k, k_bias, v_t, gate)


def kernel(x, ffn1_pre_g, ffn1_post_g, ffn1_w_in, ffn1_w_out, mix_pre_g, mix_post_g,
           ffn2_pre_g, ffn2_post_g, ffn2_w_in, ffn2_w_out, conv_w_in, conv_k, conv_w_out,
           kv_g, kv_w, forget_b, attn_w_qg, attn_w_o):
    b, s, d = x.shape
    w1_in, w1_out = _to_bf16(ffn1_w_in), _to_bf16(ffn1_w_out)
    w2_in, w2_out = _to_bf16(ffn2_w_in), _to_bf16(ffn2_w_out)

    (h,) = _ffn(x.reshape(b * s, d), ffn1_pre_g[0], ffn1_post_g[0], w1_in, w1_out, 0)
    h = _conv_mixer(h.reshape(b, s, d), mix_pre_g[0], mix_post_g[0], _to_bf16(conv_w_in),
                    conv_k[0], _to_bf16(conv_w_out))
    (h,) = _ffn(h.reshape(b * s, d), ffn2_pre_g[0], ffn2_post_g[0], w2_in, w2_out, 0)

    w_f = jnp.pad(kv_w[:, 2 * d:], ((0, 0), (0, LANES - N_HEADS))).astype(jnp.bfloat16)
    fb = jnp.pad(forget_b, (0, LANES - N_HEADS)).reshape(1, LANES)
    h, k, v_t, k_bias, q_bias, q, gate = _ffn(
        h, ffn1_pre_g[1], ffn1_post_g[1], w1_in, w1_out, 1, tm=FFN_SUB, seq_len=s,
        kv=(kv_g, _to_bf16(kv_w[None], cols=2 * d), w_f, fb),
        qg=(mix_pre_g[1], _to_bf16(attn_w_qg)))

    o = _attention(q, q_bias, k, k_bias, v_t, gate.reshape(b, s, d))
    (h,) = _ffn(h, ffn2_pre_g[1], ffn2_post_g[1], w2_in, w2_out, 1,
                attn_out=(o.reshape(b * s, d), _to_bf16(attn_w_o), mix_post_g[1]))
    return h.reshape(b, s, d)
```

```python
import math

import jax
import jax.numpy as jnp
from jax import lax
from jax.experimental import pallas as pl
from jax.experimental.pallas import tpu as pltpu

D_MODEL = 1024
N_HEADS = 16
HEAD_DIM = D_MODEL // N_HEADS
N_PAIRS = N_HEADS // 2
D_FF = 2816
RMS_EPS = 1e-6
LANES = 128
MXU_COLS = 256
VMEM_LIMIT = 56 * 1024 * 1024
CAST_BLOCK_BYTES = 4 * 1024 * 1024

FFN_TM = 1024
FFN_SUB = 512
FFN_FC = 256
MIX_TS = 512
MIX_CC = 256
ATT_T = 256
ATT_PAIRS = 2
ONES_ROWS = 16
LOG2_E = math.log2(math.e)
SOFTMAX_LAG = 2
VALUE_LAG = 5

BIAS_MID, BIAS_LO, BIAS_Q = 16, 32, 64


def _rms(x, g):
    return x * lax.rsqrt(jnp.mean(x * x, axis=-1, keepdims=True) + RMS_EPS) * g


def _dot(a, b):
    return jnp.dot(a, b, preferred_element_type=jnp.float32)


def _resident(shape):
    nd = len(shape)
    return pl.BlockSpec(shape, lambda *_: (0,) * nd, pipeline_mode=pl.Buffered(1))


def _resident_layer(shape, layer):
    return pl.BlockSpec((None,) + tuple(shape), lambda *_: (layer, 0, 0),
                        pipeline_mode=pl.Buffered(1))


def _cast_kernel(w_ref, o_ref):
    o_ref[...] = w_ref[...].astype(jnp.bfloat16)


def _to_bf16(w, cols=None):
    layers, rows, all_cols = w.shape
    cols = all_cols if cols is None else cols
    block_rows = rows
    while block_rows * cols * 4 > CAST_BLOCK_BYTES and block_rows % 32 == 0:
        block_rows //= 2
    spec = pl.BlockSpec((None, block_rows, cols), lambda l, i: (l, i, 0))
    return pl.pallas_call(
        _cast_kernel,
        name="to_bf16",
        grid=(layers, rows // block_rows),
        in_specs=[spec],
        out_specs=spec,
        out_shape=jax.ShapeDtypeStruct((layers, rows, cols), jnp.bfloat16),
        compiler_params=pltpu.CompilerParams(
            dimension_semantics=("arbitrary", "arbitrary"), vmem_limit_bytes=VMEM_LIMIT),
    )(w)


def _split3(a):
    hi = a.astype(jnp.bfloat16)
    r = a - hi.astype(jnp.float32)
    mid = r.astype(jnp.bfloat16)
    lo = (r - mid.astype(jnp.float32)).astype(jnp.bfloat16)
    return hi, mid, lo


def _make_ffn_kernel(n_sub, attn_out, kv, qg, seq_tiles):
    n_chunks = D_FF // FFN_FC
    groups = D_MODEL // MXU_COLS
    halves = MXU_COLS // LANES
    assert n_sub == 1 or not (kv or qg)

    def body(*refs):
        refs = list(refs)

        def take(n):
            return [refs.pop(0) for _ in range(n)]

        (x_ref,) = take(1)
        if attn_out:
            og_ref, wo_ref, mixpost_ref = take(3)
        pre_ref, post_ref, win_ref, wout_ref = take(4)
        if kv:
            kvg_ref, wkv_ref, wf_ref, fb_ref = take(4)
        if qg:
            mixpre_ref, wqg_ref = take(2)
        (o_ref,) = take(1)
        if kv:
            k_ref, vt_ref, kb_ref, qb_ref = take(4)
        if qg:
            q_ref, gate_ref = take(2)
        if kv:
            run_ref, tri_ref = take(2)

            @pl.when(pl.program_id(0) % seq_tiles == 0)
            def _():
                run_ref[...] = jnp.zeros_like(run_ref)

            @pl.when(pl.program_id(0) == 0)
            def _():
                r_idx = lax.broadcasted_iota(jnp.int32, tri_ref.shape, 0)
                c_idx = lax.broadcasted_iota(jnp.int32, tri_ref.shape, 1)
                tri_ref[...] = (c_idx <= r_idx).astype(jnp.bfloat16)

        x_in, xn, x_kv, x_q, acc, stash = {}, {}, {}, {}, {}, {}

        def rows(r):
            return slice(r * FFN_SUB, (r + 1) * FFN_SUB)

        def normalize(r, x):
            x_in[r] = x
            xhat = x * lax.rsqrt(jnp.mean(x * x, axis=-1, keepdims=True) + RMS_EPS)
            xn[r] = (xhat * pre_ref[...]).astype(jnp.bfloat16)
            if kv:
                x_kv[r] = (xhat * kvg_ref[...]).astype(jnp.bfloat16)
            acc[r] = jnp.zeros((FFN_SUB, D_MODEL), jnp.float32)

        def attn_out_items(r):
            parts = []

            def make(c):
                def project():
                    parts.append(_dot(og_ref[rows(r), :],
                                      wo_ref[:, c * MXU_COLS:(c + 1) * MXU_COLS]))

                def finish():
                    if c == groups - 1:
                        mixed = jnp.concatenate(parts, axis=1)
                        normalize(r, x_ref[rows(r), :] + _rms(mixed, mixpost_ref[...]))
                return project, None, finish
            return [make(c) for c in range(groups)]

        def ffn_items(r):
            def make(c):
                lo = c * FFN_FC
                key = ("ffn", r, c)

                def project():
                    if c == 0 and not attn_out:
                        normalize(r, x_ref[rows(r), :])
                    stash[key] = (_dot(xn[r], win_ref[:, lo:lo + FFN_FC]),
                                  _dot(xn[r], win_ref[:, D_FF + lo:D_FF + lo + FFN_FC]))

                def activate():
                    gate, up = stash.pop(key)
                    stash[key] = (gate * jax.nn.sigmoid(gate) * up).astype(jnp.bfloat16)

                def output():
                    acc[r] = acc[r] + _dot(stash.pop(key), wout_ref[lo:lo + FFN_FC, :])
                    if c == n_chunks - 1:
                        out = x_in.pop(r) + 0.5 * _rms(acc.pop(r), post_ref[...])
                        o_ref[rows(r), :] = out
                        if qg:
                            x_q[r] = _rms(out, mixpre_ref[...]).astype(jnp.bfloat16)
                return project, activate, output
            return [make(c) for c in range(n_chunks)]

        def kv_items(r):
            ts = FFN_SUB
            lane = lax.broadcasted_iota(jnp.int32, (1, LANES), 1)
            is_head = lane < N_HEADS

            def logit():
                stash["f"] = _dot(x_kv[r], wf_ref[...]) + fb_ref[...]

            def log_forget():
                f_logit = stash.pop("f")
                log_f = jnp.minimum(f_logit, 0.0) - jnp.log1p(jnp.exp(-jnp.abs(f_logit)))
                hi, mid, lo = (part.astype(jnp.float32)
                               for part in _split3(jnp.where(is_head, log_f, 0.0)))
                packed = hi + pltpu.roll(mid, BIAS_MID, axis=1) + pltpu.roll(lo, BIAS_LO, axis=1)
                stash["f"] = packed.astype(jnp.bfloat16)

            def cumulative():
                sums = _dot(tri_ref[...], stash.pop("f"))
                sums = (sums + pltpu.roll(sums, LANES - BIAS_MID, axis=1)
                        + pltpu.roll(sums, LANES - BIAS_LO, axis=1))
                csum = jnp.where(is_head, sums + run_ref[0:1, :], 0.0)
                run_ref[...] = jnp.broadcast_to(csum[ts - 1:ts, :], run_ref.shape)
                hi, mid, lo = (part.astype(jnp.float32) for part in _split3(csum * LOG2_E))
                key_bias = (jnp.where(lane >= BIAS_Q, 1.0, -hi)
                            - pltpu.roll(mid, BIAS_MID, axis=1) - pltpu.roll(lo, BIAS_LO, axis=1))
                query_bias = (pltpu.roll(hi, BIAS_Q, axis=1)
                              + pltpu.roll(mid, BIAS_Q + BIAS_MID, axis=1)
                              + pltpu.roll(lo, BIAS_Q + BIAS_LO, axis=1))
                kb_ref[0] = key_bias.astype(jnp.bfloat16)
                qb_ref[0] = query_bias.astype(jnp.bfloat16)

            def make(c):
                key = ("kv", c)

                def project():
                    stash[key] = (
                        _dot(x_kv[r], wkv_ref[:, c * MXU_COLS:(c + 1) * MXU_COLS]),
                        _dot(x_kv[r],
                             wkv_ref[:, D_MODEL + c * MXU_COLS:D_MODEL + (c + 1) * MXU_COLS]))

                def store():
                    k2, v2 = stash.pop(key)
                    for half in range(halves):
                        p = c * halves + half
                        k_ref[0, p] = k2[:, half * LANES:(half + 1) * LANES].astype(jnp.bfloat16)
                        for blk in range(ts // ATT_T):
                            vt_ref[0, p, blk] = v2[
                                blk * ATT_T:(blk + 1) * ATT_T,
                                half * LANES:(half + 1) * LANES].T.astype(jnp.bfloat16)
                return project, None, store
            return [(logit, log_forget, cumulative)] + [make(c) for c in range(groups)]

        def qg_items(r):
            scale = LOG2_E / math.sqrt(HEAD_DIM)

            def make(c):
                key = ("qg", c)

                def project():
                    stash[key] = _dot(x_q[r], wqg_ref[:, c * MXU_COLS:(c + 1) * MXU_COLS])

                def store():
                    val = stash.pop(key)
                    if c >= groups:
                        gate_ref[:, (c - groups) * MXU_COLS:(c - groups + 1) * MXU_COLS] = val
                        return
                    val = val * scale
                    for half in range(halves):
                        q_ref[0, c * halves + half] = (
                            val[:, half * LANES:(half + 1) * LANES].astype(jnp.bfloat16))
                return project, None, store
            return [make(c) for c in range(2 * groups)]

        bubble = (None, None, None)
        items = []
        if attn_out:
            items += attn_out_items(0) + [bubble, bubble]
        for r in range(n_sub):
            ffn = ffn_items(r)
            if attn_out and r + 1 < n_sub:
                ffn = ffn[:n_chunks // 2] + attn_out_items(r + 1) + ffn[n_chunks // 2:]
            items += ffn
        if kv:
            items += kv_items(0)
        if qg:
            items += ([] if kv else [bubble, bubble]) + qg_items(0)

        for i in range(len(items) + 2):
            for lag in range(3):
                if 0 <= i - lag < len(items) and items[i - lag][lag] is not None:
                    items[i - lag][lag]()

    return body


def _ffn(x2d, pre_g, post_g, w_in, w_out, layer, *, tm=FFN_TM, attn_out=None, kv=None, qg=None,
         seq_len=None):
    t, d = x2d.shape
    row_spec = pl.BlockSpec((tm, d), lambda i: (i, 0))
    vec_spec = _resident((1, d))
    operands, in_specs = [x2d], [row_spec]
    if attn_out is not None:
        og, w_o, mix_post_g = attn_out
        operands += [og, w_o, mix_post_g.reshape(1, -1)]
        in_specs += [row_spec, _resident_layer((d, d), 0), vec_spec]
    operands += [pre_g.reshape(1, -1), post_g.reshape(1, -1), w_in, w_out]
    in_specs += [vec_spec, vec_spec, _resident_layer((d, 2 * D_FF), layer),
                 _resident_layer((D_FF, d), layer)]
    out_shape = [jax.ShapeDtypeStruct((t, d), jnp.float32)]
    out_specs = [row_spec]
    scratch = []
    seq_tiles = None
    if kv is not None or qg is not None:
        seq_tiles = seq_len // tm
        bsz = t // seq_len
        pair_shape = jax.ShapeDtypeStruct((bsz, N_PAIRS, seq_len, LANES), jnp.bfloat16)
        pair_spec = pl.BlockSpec((1, N_PAIRS, tm, LANES),
                                 lambda i: (i // seq_tiles, 0, i % seq_tiles, 0))
    if kv is not None:
        kv_g, w_kv, w_f, fb = kv
        operands += [kv_g.reshape(1, -1), w_kv, w_f, fb]
        in_specs += [vec_spec, _resident_layer((d, 2 * d), 0), _resident((d, LANES)),
                     _resident((1, LANES))]
        bias_shape = jax.ShapeDtypeStruct((bsz, seq_len, LANES), jnp.bfloat16)
        bias_spec = pl.BlockSpec((1, tm, LANES), lambda i: (i // seq_tiles, i % seq_tiles, 0))
        out_shape += [
            pair_shape,
            jax.ShapeDtypeStruct((bsz, N_PAIRS, seq_len // ATT_T, LANES, ATT_T), jnp.bfloat16),
            bias_shape, bias_shape]
        out_specs += [
            pair_spec,
            pl.BlockSpec((1, N_PAIRS, tm // ATT_T, LANES, ATT_T),
                         lambda i: (i // seq_tiles, 0, i % seq_tiles, 0, 0)),
            bias_spec, bias_spec]
        scratch += [pltpu.VMEM((8, LANES), jnp.float32),
                    pltpu.VMEM((tm, tm), jnp.bfloat16)]
    if qg is not None:
        mix_pre_g, w_qg = qg
        operands += [mix_pre_g.reshape(1, -1), w_qg]
        in_specs += [vec_spec, _resident_layer((d, 2 * d), 0)]
        out_shape += [pair_shape, jax.ShapeDtypeStruct((t, d), jnp.float32)]
        out_specs += [pair_spec, row_spec]
    return pl.pallas_call(
        _make_ffn_kernel(tm // FFN_SUB, attn_out is not None, kv is not None, qg is not None,
                         seq_tiles),
        name="ffn",
        grid=(t // tm,),
        in_specs=in_specs,
        out_specs=out_specs,
        out_shape=out_shape,
        scratch_shapes=scratch,
        compiler_params=pltpu.CompilerParams(
            dimension_semantics=("arbitrary",), vmem_limit_bytes=VMEM_LIMIT),
    )(*operands)


def _mix_kernel(x_ref, pre_ref, post_ref, win_ref, ck_ref, wout_ref, o_ref, carry_ref):
    @pl.when(pl.program_id(1) == 0)
    def _():
        carry_ref[...] = jnp.zeros_like(carry_ref)

    x = x_ref[0]
    ts = x.shape[0]
    xn = _rms(x, pre_ref[...]).astype(jnp.bfloat16)
    row = lax.broadcasted_iota(jnp.int32, (ts, MIX_CC), 0)
    n_chunks = D_MODEL // MIX_CC
    proj, gated = {}, {}
    acc = [jnp.zeros(x.shape, jnp.float32)]

    def project(c):
        lo = c * MIX_CC
        proj[c] = tuple(_dot(xn, win_ref[:, part * D_MODEL + lo:part * D_MODEL + lo + MIX_CC])
                        for part in range(3))

    def conv(c):
        lo = c * MIX_CC
        b_gate, c_gate, h = proj.pop(c)
        u = c_gate * h
        prev1 = carry_ref[7:8, lo:lo + MIX_CC]
        prev2 = carry_ref[6:7, lo:lo + MIX_CC]
        u1 = jnp.where(row == 0, prev1, pltpu.roll(u, 1, axis=0))
        u2 = jnp.where(row == 0, prev2, jnp.where(row == 1, prev1, pltpu.roll(u, 2, axis=0)))
        carry_ref[:, lo:lo + MIX_CC] = u[ts - 8:, :]
        y = (ck_ref[0:1, lo:lo + MIX_CC] * u2 + ck_ref[1:2, lo:lo + MIX_CC] * u1
             + ck_ref[2:3, lo:lo + MIX_CC] * u)
        gated[c] = (b_gate * y).astype(jnp.bfloat16)

    def output(c):
        lo = c * MIX_CC
        acc[0] = acc[0] + _dot(gated.pop(c), wout_ref[lo:lo + MIX_CC, :])

    for step in range(n_chunks + 2):
        if step < n_chunks:
            project(step)
        if 1 <= step <= n_chunks:
            conv(step - 1)
        if step >= 2:
            output(step - 2)
    o_ref[0] = x + _rms(acc[0], post_ref[...])


def _conv_mixer(x, pre_g, post_g, w_in, conv_k, w_out):
    b, s, d = x.shape
    return pl.pallas_call(
        _mix_kernel,
        name="conv_mixer",
        grid=(b, s // MIX_TS),
        in_specs=[
            pl.BlockSpec((1, MIX_TS, d), lambda i, j: (i, j, 0)),
            _resident((1, d)),
            _resident((1, d)),
            _resident_layer((d, 3 * d), 0),
            _resident((3, d)),
            _resident_layer((d, d), 0),
        ],
        out_specs=pl.BlockSpec((1, MIX_TS, d), lambda i, j: (i, j, 0)),
        out_shape=jax.ShapeDtypeStruct(x.shape, jnp.float32),
        scratch_shapes=[pltpu.VMEM((8, d), jnp.float32)],
        compiler_params=pltpu.CompilerParams(
            dimension_semantics=("arbitrary", "arbitrary"), vmem_limit_bytes=VMEM_LIMIT),
    )(x, pre_g.reshape(1, -1), post_g.reshape(1, -1), w_in, conv_k, w_out)


def _attn_kernel(q_ref, qb_ref, k_ref, kb_ref, vt_ref, gate_ref, o_ref, qa_ref, m_ref, acc_ref):
    t = ATT_T
    n_q = q_ref.shape[2] // t
    lane = lax.broadcasted_iota(jnp.int32, (1, LANES), 1)
    key_idx = lax.broadcasted_iota(jnp.int32, (t, t), 0)
    query_idx = lax.broadcasted_iota(jnp.int32, (t, t), 1)
    causal = key_idx <= query_idx
    ones_rows = jnp.ones((ONES_ROWS, t), jnp.bfloat16)

    def build_query_operands(pp):
        pair = pl.program_id(1) * ATT_PAIRS + pp
        for j in range(2):
            h = 2 * pair + j
            head_mask = (lane >= j * HEAD_DIM) & (lane < (j + 1) * HEAD_DIM)
            bias_mask = ((lane == BIAS_Q + h) | (lane == BIAS_Q + BIAS_MID + h)
                         | (lane == BIAS_Q + BIAS_LO + h))
            bias_ones = jnp.where((lane == h) | (lane == BIAS_MID + h) | (lane == BIAS_LO + h),
                                  1.0, 0.0).astype(jnp.bfloat16)
            for qi in range(n_q):
                q2 = q_ref[0, pp, qi * t:(qi + 1) * t, :]
                qb = qb_ref[0, qi * t:(qi + 1) * t, :]
                qa_ref[pp, qi, j] = jnp.concatenate(
                    [jnp.where(head_mask, q2, jnp.zeros_like(q2)),
                     jnp.where(bias_mask, qb, bias_ones)], axis=1).T

    chains = [(pp, kt, qi, j) for pp in range(ATT_PAIRS) for kt in range(n_q)
              for qi in range(kt, n_q) for j in range(2)]
    per_pair = len(chains) // ATT_PAIRS
    scores, probs, o_t = {}, {}, {}

    def score_stage(i):
        pp, kt, qi, j = chains[i]
        if i % per_pair == per_pair // 2 and pp + 1 < ATT_PAIRS:
            build_query_operands(pp + 1)
        ka = jnp.concatenate([k_ref[0, pp, kt * t:(kt + 1) * t, :],
                              kb_ref[0, kt * t:(kt + 1) * t, :]], axis=1)
        scores[i] = _dot(ka, qa_ref[pp, qi, j])

    def softmax_stage(i):
        pp, kt, qi, j = chains[i]
        s = scores.pop(i)
        if qi == kt:
            s = jnp.where(causal, s, -jnp.inf)
        m_new = jnp.max(s, axis=0, keepdims=True)
        alpha = None
        if kt > 0:
            m_old = m_ref[pp, qi, j]
            m_new = jnp.maximum(m_old, m_new)
            alpha = jnp.exp2(m_old - m_new)
        if qi > kt:
            m_ref[pp, qi, j] = m_new
        probs[i] = (jnp.exp2(s - m_new).astype(jnp.bfloat16), alpha)

    def value_stage(i):
        pp, kt, qi, j = chains[i]
        p, alpha = probs.pop(i)
        v_aug = jnp.concatenate(
            [vt_ref[0, pp, kt, j * HEAD_DIM:(j + 1) * HEAD_DIM, :], ones_rows], axis=0)
        acc = _dot(v_aug, p)
        if kt > 0:
            acc = alpha * acc_ref[pp, qi, j] + acc
        if qi > kt:
            acc_ref[pp, qi, j] = acc
            return
        o_t[j] = acc[:HEAD_DIM] / acc[HEAD_DIM:HEAD_DIM + 1]
        if j == 1:
            o = jnp.concatenate([o_t.pop(0), o_t.pop(1)], axis=0).T
            cols = slice(pp * LANES, (pp + 1) * LANES)
            gate = gate_ref[0, qi * t:(qi + 1) * t, cols]
            o_ref[0, qi * t:(qi + 1) * t, cols] = (jax.nn.sigmoid(gate) * o).astype(jnp.bfloat16)

    build_query_operands(0)
    n = len(chains)
    for i in range(n + VALUE_LAG):
        if i < n:
            score_stage(i)
        if SOFTMAX_LAG <= i < n + SOFTMAX_LAG:
            softmax_stage(i - SOFTMAX_LAG)
        if i >= VALUE_LAG:
            value_stage(i - VALUE_LAG)


def _attention(q, q_bias, k, k_bias, v_t, gate):
    b, _, s, _ = q.shape
    n_q = s // ATT_T
    qk_spec = pl.BlockSpec((1, ATT_PAIRS, s, LANES), lambda i, g: (i, g, 0, 0))
    bias_spec = pl.BlockSpec((1, s, LANES), lambda i, g: (i, 0, 0))
    slab_spec = pl.BlockSpec((1, s, ATT_PAIRS * LANES), lambda i, g: (i, 0, g))
    return pl.pallas_call(
        _attn_kernel,
        name="fox_attention",
        grid=(b, N_PAIRS // ATT_PAIRS),
        in_specs=[
            qk_spec, bias_spec, qk_spec, bias_spec,
            pl.BlockSpec((1, ATT_PAIRS, s // ATT_T, LANES, ATT_T), lambda i, g: (i, g, 0, 0, 0)),
            slab_spec,
        ],
        out_specs=slab_spec,
        out_shape=jax.ShapeDtypeStruct((b, s, D_MODEL), jnp.bfloat16),
        scratch_shapes=[
            pltpu.VMEM((ATT_PAIRS, n_q, 2, 2 * LANES, ATT_T), jnp.bfloat16),
            pltpu.VMEM((ATT_PAIRS, n_q, 2, 1, ATT_T), jnp.float32),
            pltpu.VMEM((ATT_PAIRS, n_q, 2, HEAD_DIM + ONES_ROWS, ATT_T), jnp.float32),
        ],
        compiler_params=pltpu.CompilerParams(
            dimension_semantics=("arbitrary", "arbitrary"), vmem_limit_bytes=VMEM_LIMIT),
    )(q, q_bias, k, k_bias, v_t, gate)


def kernel(x, ffn1_pre_g, ffn1_post_g, ffn1_w_in, ffn1_w_out, mix_pre_g, mix_post_g,
           ffn2_pre_g, ffn2_post_g, ffn2_w_in, ffn2_w_out, conv_w_in, conv_k, conv_w_out,
           kv_g, kv_w, forget_b, attn_w_qg, attn_w_o):
    b, s, d = x.shape
    w1_in, w1_out = _to_bf16(ffn1_w_in), _to_bf16(ffn1_w_out)
    w2_in, w2_out = _to_bf16(ffn2_w_in), _to_bf16(ffn2_w_out)

    (h,) = _ffn(x.reshape(b * s, d), ffn1_pre_g[0], ffn1_post_g[0], w1_in, w1_out, 0)
    h = _conv_mixer(h.reshape(b, s, d), mix_pre_g[0], mix_post_g[0], _to_bf16(conv_w_in),
                    conv_k[0], _to_bf16(conv_w_out))
    (h,) = _ffn(h.reshape(b * s, d), ffn2_pre_g[0], ffn2_post_g[0], w2_in, w2_out, 0)

    w_f = jnp.pad(kv_w[:, 2 * d:], ((0, 0), (0, LANES - N_HEADS))).astype(jnp.bfloat16)
    fb = jnp.pad(forget_b, (0, LANES - N_HEADS)).reshape(1, LANES)
    h, k, v_t, k_bias, q_bias, q, gate = _ffn(
        h, ffn1_pre_g[1], ffn1_post_g[1], w1_in, w1_out, 1, tm=FFN_SUB, seq_len=s,
        kv=(kv_g, _to_bf16(kv_w[None], cols=2 * d), w_f, fb),
        qg=(mix_pre_g[1], _to_bf16(attn_w_qg)))

    o = _attention(q, q_bias, k, k_bias, v_t, gate.reshape(b, s, d))
    (h,) = _ffn(h, ffn2_pre_g[1], ffn2_post_g[1], w2_in, w2_out, 1,
                attn_out=(o.reshape(b * s, d), _to_bf16(attn_w_o), mix_post_g[1]))
    return h.reshape(b, s, d)
```

```python
import math

import jax
import jax.numpy as jnp
from jax import lax
from jax.experimental import pallas as pl
from jax.experimental.pallas import tpu as pltpu

D_MODEL = 1024
N_HEADS = 16
HEAD_DIM = D_MODEL // N_HEADS
N_PAIRS = N_HEADS // 2
D_FF = 2816
RMS_EPS = 1e-6
LANES = 128
MXU_COLS = 256
VMEM_LIMIT = 56 * 1024 * 1024
CAST_BLOCK_BYTES = 4 * 1024 * 1024

FFN_TM = 1024
FFN_SUB = 512
FFN_FC = 256
MIX_TS = 512
MIX_CC = 256
ATT_T = 256
ATT_PAIRS = 2
ONES_ROWS = 16
LOG2_E = math.log2(math.e)
SOFTMAX_LAG = 2
VALUE_LAG = 5

BIAS_MID, BIAS_LO, BIAS_Q = 16, 32, 64


def _rms(x, g):
    return x * lax.rsqrt(jnp.mean(x * x, axis=-1, keepdims=True) + RMS_EPS) * g


def _dot(a, b):
    return jnp.dot(a, b, preferred_element_type=jnp.float32)


def _resident(shape):
    nd = len(shape)
    return pl.BlockSpec(shape, lambda *_: (0,) * nd, pipeline_mode=pl.Buffered(1))


def _resident_weight(shape):
    return pl.BlockSpec((None,) + tuple(shape), lambda *_: (0, 0, 0),
                        pipeline_mode=pl.Buffered(1))


def _cast_kernel(w_ref, o_ref):
    o_ref[...] = w_ref[...].astype(jnp.bfloat16)


def _to_bf16(w, layer):
    _, rows, cols = w.shape
    block_rows = rows
    while block_rows * cols * 4 > CAST_BLOCK_BYTES and block_rows % 32 == 0:
        block_rows //= 2
    return pl.pallas_call(
        _cast_kernel,
        name="to_bf16",
        grid=(rows // block_rows,),
        in_specs=[pl.BlockSpec((None, block_rows, cols), lambda i: (layer, i, 0))],
        out_specs=pl.BlockSpec((None, block_rows, cols), lambda i: (0, i, 0)),
        out_shape=jax.ShapeDtypeStruct((1, rows, cols), jnp.bfloat16),
        compiler_params=pltpu.CompilerParams(
            dimension_semantics=("arbitrary",), vmem_limit_bytes=VMEM_LIMIT),
    )(w)


def _side_cast(request, n_steps, step_of):
    w, layer, cols = request
    rows = w.shape[-2]
    cols = w.shape[-1] if cols is None else cols
    parts = n_steps
    while rows % parts or (rows // parts) % 16:
        parts //= 2
    block_rows = rows // parts

    def part(*ids):
        return jnp.minimum(step_of(*ids), parts - 1)

    if w.ndim == 2:
        in_spec = pl.BlockSpec((block_rows, cols), lambda *ids: (part(*ids), 0))
    else:
        in_spec = pl.BlockSpec((None, block_rows, cols), lambda *ids: (layer, part(*ids), 0))
    out_spec = pl.BlockSpec((None, block_rows, cols), lambda *ids: (0, part(*ids), 0))
    return in_spec, out_spec, jax.ShapeDtypeStruct((1, rows, cols), jnp.bfloat16)


def _run_side_casts(in_refs, out_refs):
    for w_ref, o_ref in zip(in_refs, out_refs):
        o_ref[...] = w_ref[...].astype(jnp.bfloat16)


def _split3(a):
    hi = a.astype(jnp.bfloat16)
    r = a - hi.astype(jnp.float32)
    mid = r.astype(jnp.bfloat16)
    lo = (r - mid.astype(jnp.float32)).astype(jnp.bfloat16)
    return hi, mid, lo


def _make_ffn_kernel(n_sub, attn_out, kv, qg, seq_tiles, n_side):
    n_chunks = D_FF // FFN_FC
    groups = D_MODEL // MXU_COLS
    halves = MXU_COLS // LANES
    assert n_sub == 1 or not (kv or qg)

    def body(*refs):
        refs = list(refs)

        def take(n):
            return [refs.pop(0) for _ in range(n)]

        (x_ref,) = take(1)
        if attn_out:
            og_ref, wo_ref, mixpost_ref = take(3)
        pre_ref, post_ref, win_ref, wout_ref = take(4)
        if kv:
            kvg_ref, wkv_ref, wf_ref, fb_ref = take(4)
        if qg:
            mixpre_ref, wqg_ref = take(2)
        side_in = take(n_side)
        (o_ref,) = take(1)
        if kv:
            k_ref, vt_ref, kb_ref, qb_ref = take(4)
        if qg:
            q_ref, gate_ref = take(2)
        side_out = take(n_side)
        if kv:
            run_ref, tri_ref = take(2)

            @pl.when(pl.program_id(0) % seq_tiles == 0)
            def _():
                run_ref[...] = jnp.zeros_like(run_ref)

            @pl.when(pl.program_id(0) == 0)
            def _():
                r_idx = lax.broadcasted_iota(jnp.int32, tri_ref.shape, 0)
                c_idx = lax.broadcasted_iota(jnp.int32, tri_ref.shape, 1)
                tri_ref[...] = (c_idx <= r_idx).astype(jnp.bfloat16)

        x_in, xn, x_kv, x_q, acc, stash = {}, {}, {}, {}, {}, {}

        def rows(r):
            return slice(r * FFN_SUB, (r + 1) * FFN_SUB)

        def normalize(r, x):
            x_in[r] = x
            xhat = x * lax.rsqrt(jnp.mean(x * x, axis=-1, keepdims=True) + RMS_EPS)
            xn[r] = (xhat * pre_ref[...]).astype(jnp.bfloat16)
            if kv:
                x_kv[r] = (xhat * kvg_ref[...]).astype(jnp.bfloat16)
            acc[r] = jnp.zeros((FFN_SUB, D_MODEL), jnp.float32)

        def attn_out_items(r):
            parts = []

            def make(c):
                def project():
                    parts.append(_dot(og_ref[rows(r), :],
                                      wo_ref[:, c * MXU_COLS:(c + 1) * MXU_COLS]))

                def finish():
                    if c == groups - 1:
                        mixed = jnp.concatenate(parts, axis=1)
                        normalize(r, x_ref[rows(r), :] + _rms(mixed, mixpost_ref[...]))
                return project, None, finish
            return [make(c) for c in range(groups)]

        def ffn_items(r):
            def make(c):
                lo = c * FFN_FC
                key = ("ffn", r, c)

                def project():
                    if c == 0 and not attn_out:
                        normalize(r, x_ref[rows(r), :])
                    stash[key] = (_dot(xn[r], win_ref[:, lo:lo + FFN_FC]),
                                  _dot(xn[r], win_ref[:, D_FF + lo:D_FF + lo + FFN_FC]))

                def activate():
                    gate, up = stash.pop(key)
                    stash[key] = (gate * jax.nn.sigmoid(gate) * up).astype(jnp.bfloat16)

                def output():
                    acc[r] = acc[r] + _dot(stash.pop(key), wout_ref[lo:lo + FFN_FC, :])
                    if c == n_chunks - 1:
                        out = x_in.pop(r) + 0.5 * _rms(acc.pop(r), post_ref[...])
                        o_ref[rows(r), :] = out
                        if qg:
                            x_q[r] = _rms(out, mixpre_ref[...]).astype(jnp.bfloat16)
                return project, activate, output
            return [make(c) for c in range(n_chunks)]

        def kv_items(r):
            ts = FFN_SUB
            lane = lax.broadcasted_iota(jnp.int32, (1, LANES), 1)
            is_head = lane < N_HEADS

            def logit():
                stash["f"] = _dot(x_kv[r], wf_ref[...]) + fb_ref[...]

            def log_forget():
                f_logit = stash.pop("f")
                log_f = jnp.minimum(f_logit, 0.0) - jnp.log1p(jnp.exp(-jnp.abs(f_logit)))
                hi, mid, lo = (part.astype(jnp.float32)
                               for part in _split3(jnp.where(is_head, log_f, 0.0)))
                packed = hi + pltpu.roll(mid, BIAS_MID, axis=1) + pltpu.roll(lo, BIAS_LO, axis=1)
                stash["f"] = packed.astype(jnp.bfloat16)

            def cumulative():
                sums = _dot(tri_ref[...], stash.pop("f"))
                sums = (sums + pltpu.roll(sums, LANES - BIAS_MID, axis=1)
                        + pltpu.roll(sums, LANES - BIAS_LO, axis=1))
                csum = jnp.where(is_head, sums + run_ref[0:1, :], 0.0)
                run_ref[...] = jnp.broadcast_to(csum[ts - 1:ts, :], run_ref.shape)
                hi, mid, lo = (part.astype(jnp.float32) for part in _split3(csum * LOG2_E))
                key_bias = (jnp.where(lane >= BIAS_Q, 1.0, -hi)
                            - pltpu.roll(mid, BIAS_MID, axis=1) - pltpu.roll(lo, BIAS_LO, axis=1))
                query_bias = (pltpu.roll(hi, BIAS_Q, axis=1)
                              + pltpu.roll(mid, BIAS_Q + BIAS_MID, axis=1)
                              + pltpu.roll(lo, BIAS_Q + BIAS_LO, axis=1))
                kb_ref[0] = key_bias.astype(jnp.bfloat16)
                qb_ref[0] = query_bias.astype(jnp.bfloat16)

            def make(c):
                key = ("kv", c)

                def project():
                    stash[key] = (
                        _dot(x_kv[r], wkv_ref[:, c * MXU_COLS:(c + 1) * MXU_COLS]),
                        _dot(x_kv[r],
                             wkv_ref[:, D_MODEL + c * MXU_COLS:D_MODEL + (c + 1) * MXU_COLS]))

                def store():
                    k2, v2 = stash.pop(key)
                    for half in range(halves):
                        p = c * halves + half
                        k_ref[0, p] = k2[:, half * LANES:(half + 1) * LANES].astype(jnp.bfloat16)
                        for blk in range(ts // ATT_T):
                            vt_ref[0, p, blk] = v2[
                                blk * ATT_T:(blk + 1) * ATT_T,
                                half * LANES:(half + 1) * LANES].T.astype(jnp.bfloat16)
                return project, None, store
            return [(logit, log_forget, cumulative)] + [make(c) for c in range(groups)]

        def qg_items(r):
            scale = LOG2_E / math.sqrt(HEAD_DIM)

            def make(c):
                key = ("qg", c)

                def project():
                    stash[key] = _dot(x_q[r], wqg_ref[:, c * MXU_COLS:(c + 1) * MXU_COLS])

                def store():
                    val = stash.pop(key)
                    if c >= groups:
                        gate_ref[:, (c - groups) * MXU_COLS:(c - groups + 1) * MXU_COLS] = val
                        return
                    val = val * scale
                    for half in range(halves):
                        q_ref[0, c * halves + half] = (
                            val[:, half * LANES:(half + 1) * LANES].astype(jnp.bfloat16))
                return project, None, store
            return [make(c) for c in range(2 * groups)]

        bubble = (None, None, None)
        items = []
        if attn_out:
            items += attn_out_items(0) + [bubble, bubble]
        for r in range(n_sub):
            ffn = ffn_items(r)
            if attn_out and r + 1 < n_sub:
                ffn = ffn[:n_chunks // 2] + attn_out_items(r + 1) + ffn[n_chunks // 2:]
            items += ffn
        if kv:
            items += kv_items(0)
        if qg:
            items += ([] if kv else [bubble, bubble]) + qg_items(0)

        for i in range(len(items) + 2):
            if i == len(items) // 2:
                _run_side_casts(side_in, side_out)
            for lag in range(3):
                if 0 <= i - lag < len(items) and items[i - lag][lag] is not None:
                    items[i - lag][lag]()

    return body


def _ffn(x2d, pre_g, post_g, w_in, w_out, *, tm=FFN_TM, attn_out=None, kv=None, qg=None,
         seq_len=None, side_casts=()):
    t, d = x2d.shape
    row_spec = pl.BlockSpec((tm, d), lambda i: (i, 0))
    vec_spec = _resident((1, d))
    operands, in_specs = [x2d], [row_spec]
    if attn_out is not None:
        og, w_o, mix_post_g = attn_out
        operands += [og, w_o, mix_post_g.reshape(1, -1)]
        in_specs += [row_spec, _resident_weight((d, d)), vec_spec]
    operands += [pre_g.reshape(1, -1), post_g.reshape(1, -1), w_in, w_out]
    in_specs += [vec_spec, vec_spec, _resident_weight((d, 2 * D_FF)),
                 _resident_weight((D_FF, d))]
    out_shape = [jax.ShapeDtypeStruct((t, d), jnp.float32)]
    out_specs = [row_spec]
    scratch = []
    seq_tiles = None
    if kv is not None or qg is not None:
        seq_tiles = seq_len // tm
        bsz = t // seq_len
        pair_shape = jax.ShapeDtypeStruct((bsz, N_PAIRS, seq_len, LANES), jnp.bfloat16)
        pair_spec = pl.BlockSpec((1, N_PAIRS, tm, LANES),
                                 lambda i: (i // seq_tiles, 0, i % seq_tiles, 0))
    if kv is not None:
        kv_g, w_kv, w_f, fb = kv
        operands += [kv_g.reshape(1, -1), w_kv, w_f, fb]
        in_specs += [vec_spec, _resident_weight((d, 2 * d)), _resident((d, LANES)),
                     _resident((1, LANES))]
        bias_shape = jax.ShapeDtypeStruct((bsz, seq_len, LANES), jnp.bfloat16)
        bias_spec = pl.BlockSpec((1, tm, LANES), lambda i: (i // seq_tiles, i % seq_tiles, 0))
        out_shape += [
            pair_shape,
            jax.ShapeDtypeStruct((bsz, N_PAIRS, seq_len // ATT_T, LANES, ATT_T), jnp.bfloat16),
            bias_shape, bias_shape]
        out_specs += [
            pair_spec,
            pl.BlockSpec((1, N_PAIRS, tm // ATT_T, LANES, ATT_T),
                         lambda i: (i // seq_tiles, 0, i % seq_tiles, 0, 0)),
            bias_spec, bias_spec]
        scratch += [pltpu.VMEM((8, LANES), jnp.float32),
                    pltpu.VMEM((tm, tm), jnp.bfloat16)]
    if qg is not None:
        mix_pre_g, w_qg = qg
        operands += [mix_pre_g.reshape(1, -1), w_qg]
        in_specs += [vec_spec, _resident_weight((d, 2 * d))]
        out_shape += [pair_shape, jax.ShapeDtypeStruct((t, d), jnp.float32)]
        out_specs += [pair_spec, row_spec]
    for request in side_casts:
        in_spec, out_spec, shape = _side_cast(request, t // tm, lambda i: i)
        operands.append(request[0])
        in_specs.append(in_spec)
        out_specs.append(out_spec)
        out_shape.append(shape)
    return pl.pallas_call(
        _make_ffn_kernel(tm // FFN_SUB, attn_out is not None, kv is not None, qg is not None,
                         seq_tiles, len(side_casts)),
        name="ffn",
        grid=(t // tm,),
        in_specs=in_specs,
        out_specs=out_specs,
        out_shape=out_shape,
        scratch_shapes=scratch,
        compiler_params=pltpu.CompilerParams(
            dimension_semantics=("arbitrary",), vmem_limit_bytes=VMEM_LIMIT),
    )(*operands)


def _mix_kernel(x_ref, pre_ref, post_ref, win_ref, ck_ref, wout_ref, *rest):
    n_side = (len(rest) - 2) // 2
    o_ref, carry_ref = rest[n_side], rest[-1]

    @pl.when(pl.program_id(1) == 0)
    def _():
        carry_ref[...] = jnp.zeros_like(carry_ref)

    x = x_ref[0]
    ts = x.shape[0]
    xn = _rms(x, pre_ref[...]).astype(jnp.bfloat16)
    row = lax.broadcasted_iota(jnp.int32, (ts, MIX_CC), 0)
    n_chunks = D_MODEL // MIX_CC
    proj, gated = {}, {}
    acc = [jnp.zeros(x.shape, jnp.float32)]

    def project(c):
        lo = c * MIX_CC
        proj[c] = tuple(_dot(xn, win_ref[:, part * D_MODEL + lo:part * D_MODEL + lo + MIX_CC])
                        for part in range(3))

    def conv(c):
        lo = c * MIX_CC
        b_gate, c_gate, h = proj.pop(c)
        u = c_gate * h
        prev1 = carry_ref[7:8, lo:lo + MIX_CC]
        prev2 = carry_ref[6:7, lo:lo + MIX_CC]
        u1 = jnp.where(row == 0, prev1, pltpu.roll(u, 1, axis=0))
        u2 = jnp.where(row == 0, prev2, jnp.where(row == 1, prev1, pltpu.roll(u, 2, axis=0)))
        carry_ref[:, lo:lo + MIX_CC] = u[ts - 8:, :]
        y = (ck_ref[0:1, lo:lo + MIX_CC] * u2 + ck_ref[1:2, lo:lo + MIX_CC] * u1
             + ck_ref[2:3, lo:lo + MIX_CC] * u)
        gated[c] = (b_gate * y).astype(jnp.bfloat16)

    def output(c):
        lo = c * MIX_CC
        acc[0] = acc[0] + _dot(gated.pop(c), wout_ref[lo:lo + MIX_CC, :])

    for step in range(n_chunks + 2):
        if step < n_chunks:
            project(step)
        if step == n_chunks - 1:
            _run_side_casts(rest[:n_side], rest[n_side + 1:-1])
        if 1 <= step <= n_chunks:
            conv(step - 1)
        if step >= 2:
            output(step - 2)
    o_ref[0] = x + _rms(acc[0], post_ref[...])


def _conv_mixer(x, pre_g, post_g, w_in, conv_k, w_out, side_casts=()):
    b, s, d = x.shape
    seq_tiles = s // MIX_TS
    x_spec = pl.BlockSpec((1, MIX_TS, d), lambda i, j: (i, j, 0))
    sides = [_side_cast(request, b * seq_tiles, lambda i, j: i * seq_tiles + j)
             for request in side_casts]
    return pl.pallas_call(
        _mix_kernel,
        name="conv_mixer",
        grid=(b, seq_tiles),
        in_specs=[
            x_spec,
            _resident((1, d)),
            _resident((1, d)),
            _resident_weight((d, 3 * d)),
            _resident((3, d)),
            _resident_weight((d, d)),
        ] + [side[0] for side in sides],
        out_specs=[x_spec] + [side[1] for side in sides],
        out_shape=[jax.ShapeDtypeStruct(x.shape, jnp.float32)] + [side[2] for side in sides],
        scratch_shapes=[pltpu.VMEM((8, d), jnp.float32)],
        compiler_params=pltpu.CompilerParams(
            dimension_semantics=("arbitrary", "arbitrary"), vmem_limit_bytes=VMEM_LIMIT),
    )(x, pre_g.reshape(1, -1), post_g.reshape(1, -1), w_in, conv_k, w_out,
      *[request[0] for request in side_casts])


def _attn_kernel(q_ref, qb_ref, k_ref, kb_ref, vt_ref, gate_ref, o_ref, qa_ref, m_ref, acc_ref):
    t = ATT_T
    n_q = q_ref.shape[2] // t
    lane = lax.broadcasted_iota(jnp.int32, (1, LANES), 1)
    key_idx = lax.broadcasted_iota(jnp.int32, (t, t), 0)
    query_idx = lax.broadcasted_iota(jnp.int32, (t, t), 1)
    causal = key_idx <= query_idx
    ones_rows = jnp.ones((ONES_ROWS, t), jnp.bfloat16)

    def build_query_operands(pp):
        pair = pl.program_id(1) * ATT_PAIRS + pp
        for j in range(2):
            h = 2 * pair + j
            head_mask = (lane >= j * HEAD_DIM) & (lane < (j + 1) * HEAD_DIM)
            bias_mask = ((lane == BIAS_Q + h) | (lane == BIAS_Q + BIAS_MID + h)
                         | (lane == BIAS_Q + BIAS_LO + h))
            bias_ones = jnp.where((lane == h) | (lane == BIAS_MID + h) | (lane == BIAS_LO + h),
                                  1.0, 0.0).astype(jnp.bfloat16)
            for qi in range(n_q):
                q2 = q_ref[0, pp, qi * t:(qi + 1) * t, :]
                qb = qb_ref[0, qi * t:(qi + 1) * t, :]
                qa_ref[pp, qi, j] = jnp.concatenate(
                    [jnp.where(head_mask, q2, jnp.zeros_like(q2)),
                     jnp.where(bias_mask, qb, bias_ones)], axis=1).T

    chains = [(pp, kt, qi, j) for pp in range(ATT_PAIRS) for kt in range(n_q)
              for qi in range(kt, n_q) for j in range(2)]
    per_pair = len(chains) // ATT_PAIRS
    scores, probs, o_t = {}, {}, {}

    def score_stage(i):
        pp, kt, qi, j = chains[i]
        if i % per_pair == per_pair // 2 and pp + 1 < ATT_PAIRS:
            build_query_operands(pp + 1)
        ka = jnp.concatenate([k_ref[0, pp, kt * t:(kt + 1) * t, :],
                              kb_ref[0, kt * t:(kt + 1) * t, :]], axis=1)
        scores[i] = _dot(ka, qa_ref[pp, qi, j])

    def softmax_stage(i):
        pp, kt, qi, j = chains[i]
        s = scores.pop(i)
        if qi == kt:
            s = jnp.where(causal, s, -jnp.inf)
        m_new = jnp.max(s, axis=0, keepdims=True)
        alpha = None
        if kt > 0:
            m_old = m_ref[pp, qi, j]
            m_new = jnp.maximum(m_old, m_new)
            alpha = jnp.exp2(m_old - m_new)
        if qi > kt:
            m_ref[pp, qi, j] = m_new
        probs[i] = (jnp.exp2(s - m_new).astype(jnp.bfloat16), alpha)

    def value_stage(i):
        pp, kt, qi, j = chains[i]
        p, alpha = probs.pop(i)
        v_aug = jnp.concatenate(
            [vt_ref[0, pp, kt, j * HEAD_DIM:(j + 1) * HEAD_DIM, :], ones_rows], axis=0)
        acc = _dot(v_aug, p)
        if kt > 0:
            acc = alpha * acc_ref[pp, qi, j] + acc
        if qi > kt:
            acc_ref[pp, qi, j] = acc
            return
        o_t[j] = acc[:HEAD_DIM] / acc[HEAD_DIM:HEAD_DIM + 1]
        if j == 1:
            o = jnp.concatenate([o_t.pop(0), o_t.pop(1)], axis=0).T
            cols = slice(pp * LANES, (pp + 1) * LANES)
            gate = gate_ref[0, qi * t:(qi + 1) * t, cols]
            o_ref[0, qi * t:(qi + 1) * t, cols] = (jax.nn.sigmoid(gate) * o).astype(jnp.bfloat16)

    build_query_operands(0)
    n = len(chains)
    for i in range(n + VALUE_LAG):
        if i < n:
            score_stage(i)
        if SOFTMAX_LAG <= i < n + SOFTMAX_LAG:
            softmax_stage(i - SOFTMAX_LAG)
        if i >= VALUE_LAG:
            value_stage(i - VALUE_LAG)


def _attention(q, q_bias, k, k_bias, v_t, gate):
    b, _, s, _ = q.shape
    n_q = s // ATT_T
    qk_spec = pl.BlockSpec((1, ATT_PAIRS, s, LANES), lambda i, g: (i, g, 0, 0))
    bias_spec = pl.BlockSpec((1, s, LANES), lambda i, g: (i, 0, 0))
    slab_spec = pl.BlockSpec((1, s, ATT_PAIRS * LANES), lambda i, g: (i, 0, g))
    return pl.pallas_call(
        _attn_kernel,
        name="fox_attention",
        grid=(b, N_PAIRS // ATT_PAIRS),
        in_specs=[
            qk_spec, bias_spec, qk_spec, bias_spec,
            pl.BlockSpec((1, ATT_PAIRS, s // ATT_T, LANES, ATT_T), lambda i, g: (i, g, 0, 0, 0)),
            slab_spec,
        ],
        out_specs=slab_spec,
        out_shape=jax.ShapeDtypeStruct((b, s, D_MODEL), jnp.bfloat16),
        scratch_shapes=[
            pltpu.VMEM((ATT_PAIRS, n_q, 2, 2 * LANES, ATT_T), jnp.bfloat16),
            pltpu.VMEM((ATT_PAIRS, n_q, 2, 1, ATT_T), jnp.float32),
            pltpu.VMEM((ATT_PAIRS, n_q, 2, HEAD_DIM + ONES_ROWS, ATT_T), jnp.float32),
        ],
        compiler_params=pltpu.CompilerParams(
            dimension_semantics=("arbitrary", "arbitrary"), vmem_limit_bytes=VMEM_LIMIT),
    )(q, q_bias, k, k_bias, v_t, gate)


def kernel(x, ffn1_pre_g, ffn1_post_g, ffn1_w_in, ffn1_w_out, mix_pre_g, mix_post_g,
           ffn2_pre_g, ffn2_post_g, ffn2_w_in, ffn2_w_out, conv_w_in, conv_k, conv_w_out,
           kv_g, kv_w, forget_b, attn_w_qg, attn_w_o):
    b, s, d = x.shape

    h, conv_in, conv_out, w2_in0, w2_out0 = _ffn(
        x.reshape(b * s, d), ffn1_pre_g[0], ffn1_post_g[0],
        _to_bf16(ffn1_w_in, 0), _to_bf16(ffn1_w_out, 0),
        side_casts=[(conv_w_in, 0, None), (conv_w_out, 0, None),
                    (ffn2_w_in, 0, None), (ffn2_w_out, 0, None)])
    h, w1_in1, w1_out1, w_kv, w_qg = _conv_mixer(
        h.reshape(b, s, d), mix_pre_g[0], mix_post_g[0], conv_in, conv_k[0], conv_out,
        side_casts=[(ffn1_w_in, 1, None), (ffn1_w_out, 1, None), (kv_w, 0, 2 * d),
                    (attn_w_qg, 0, None)])
    h, w2_in1, w2_out1, w_o = _ffn(
        h.reshape(b * s, d), ffn2_pre_g[0], ffn2_post_g[0], w2_in0, w2_out0,
        side_casts=[(ffn2_w_in, 1, None), (ffn2_w_out, 1, None), (attn_w_o, 0, None)])

    w_f = jnp.pad(kv_w[:, 2 * d:], ((0, 0), (0, LANES - N_HEADS))).astype(jnp.bfloat16)
    fb = jnp.pad(forget_b, (0, LANES - N_HEADS)).reshape(1, LANES)
    h, k, v_t, k_bias, q_bias, q, gate = _ffn(
        h, ffn1_pre_g[1], ffn1_post_g[1], w1_in1, w1_out1, tm=FFN_SUB, seq_len=s,
        kv=(kv_g, w_kv, w_f, fb), qg=(mix_pre_g[1], w_qg))

    o = _attention(q, q_bias, k, k_bias, v_t, gate.reshape(b, s, d))
    (h,) = _ffn(h, ffn2_pre_g[1], ffn2_post_g[1], w2_in1, w2_out1,
                attn_out=(o.reshape(b * s, d), w_o, mix_post_g[1]))
    return h.reshape(b, s, d)
```

```python
import math

import jax
import jax.numpy as jnp
from jax import lax
from jax.experimental import pallas as pl
from jax.experimental.pallas import tpu as pltpu

D_MODEL = 1024
N_HEADS = 16
HEAD_DIM = D_MODEL // N_HEADS
N_PAIRS = N_HEADS // 2
D_FF = 2816
RMS_EPS = 1e-6
LANES = 128
MXU_COLS = 256
VMEM_LIMIT = 56 * 1024 * 1024
CAST_BLOCK_BYTES = 4 * 1024 * 1024

FFN_TM = 1024
FFN_SUB = 512
FFN_FC = 256
ITEM_LAGS = (0, 1, 2)
MIX_TS = 1024
MIX_SUB = 512
MIX_CC = 256
ATT_T = 256
ATT_PAIRS = 2
ONES_ROWS = 16
LOG2_E = math.log2(math.e)
SOFTMAX_LAG = 2
VALUE_LAG = 5

BIAS_MID, BIAS_LO, BIAS_Q = 16, 32, 64


def _rms(x, g):
    return x * lax.rsqrt(jnp.mean(x * x, axis=-1, keepdims=True) + RMS_EPS) * g


def _dot(a, b):
    return jnp.dot(a, b, preferred_element_type=jnp.float32)


def _resident(shape):
    nd = len(shape)
    return pl.BlockSpec(shape, lambda *_: (0,) * nd, pipeline_mode=pl.Buffered(1))


def _resident_weight(shape):
    return pl.BlockSpec((None,) + tuple(shape), lambda *_: (0, 0, 0),
                        pipeline_mode=pl.Buffered(1))


def _cast_kernel(w_ref, o_ref):
    o_ref[...] = w_ref[...].astype(jnp.bfloat16)


def _to_bf16(w, layer):
    _, rows, cols = w.shape
    block_rows = rows
    while block_rows * cols * 4 > CAST_BLOCK_BYTES and block_rows % 32 == 0:
        block_rows //= 2
    return pl.pallas_call(
        _cast_kernel,
        name="to_bf16",
        grid=(rows // block_rows,),
        in_specs=[pl.BlockSpec((None, block_rows, cols), lambda i: (layer, i, 0))],
        out_specs=pl.BlockSpec((None, block_rows, cols), lambda i: (0, i, 0)),
        out_shape=jax.ShapeDtypeStruct((1, rows, cols), jnp.bfloat16),
        compiler_params=pltpu.CompilerParams(
            dimension_semantics=("arbitrary",), vmem_limit_bytes=VMEM_LIMIT),
    )(w)


def _side_cast(request, n_steps, step_of):
    w, layer, cols = request

    def part(*ids):
        return jnp.minimum(step_of(*ids), parts - 1)

    if w.ndim == 2:
        rows, parts = w.shape[1], cols // LANES
        assert parts <= n_steps
        in_spec = pl.BlockSpec((LANES, rows), lambda *ids: (part(*ids), 0))
        out_spec = pl.BlockSpec((None, rows, LANES), lambda *ids: (0, 0, part(*ids)))
        return in_spec, out_spec, jax.ShapeDtypeStruct((1, rows, cols), jnp.bfloat16)
    rows = w.shape[1]
    cols = w.shape[2] if cols is None else cols
    parts = n_steps
    while rows % parts or (rows // parts) % 16:
        parts //= 2
    block = (None, rows // parts, cols)
    in_spec = pl.BlockSpec(block, lambda *ids: (layer, part(*ids), 0))
    out_spec = pl.BlockSpec(block, lambda *ids: (0, part(*ids), 0))
    return in_spec, out_spec, jax.ShapeDtypeStruct((1, rows, cols), jnp.bfloat16)


def _run_side_casts(in_refs, out_refs):
    for w_ref, o_ref in zip(in_refs, out_refs):
        w = w_ref[...]
        o_ref[...] = (w if w.shape == o_ref.shape else w.T).astype(jnp.bfloat16)


def _split3(a):
    hi = a.astype(jnp.bfloat16)
    r = a - hi.astype(jnp.float32)
    mid = r.astype(jnp.bfloat16)
    lo = (r - mid.astype(jnp.float32)).astype(jnp.bfloat16)
    return hi, mid, lo


def _make_ffn_kernel(n_sub, attn_out, kv, qg, seq_tiles, n_side):
    n_chunks = D_FF // FFN_FC
    groups = D_MODEL // MXU_COLS
    halves = MXU_COLS // LANES
    assert n_sub == 1 or not (kv or qg)

    def body(*refs):
        refs = list(refs)

        def take(n):
            return [refs.pop(0) for _ in range(n)]

        (x_ref,) = take(1)
        if attn_out:
            og_ref, wo_ref, mixpost_ref = take(3)
        pre_ref, post_ref, win_ref, wout_ref = take(4)
        if kv:
            kvg_ref, wkv_ref, wf_ref, fb_ref = take(4)
        if qg:
            mixpre_ref, wqg_ref = take(2)
        side_in = take(n_side)
        (o_ref,) = take(1)
        if kv:
            k_ref, vt_ref, kb_ref, qb_ref = take(4)
        if qg:
            q_ref, gate_ref = take(2)
        side_out = take(n_side)
        if kv:
            run_ref, tri_ref = take(2)

            @pl.when(pl.program_id(0) % seq_tiles == 0)
            def _():
                run_ref[...] = jnp.zeros_like(run_ref)

            @pl.when(pl.program_id(0) == 0)
            def _():
                r_idx = lax.broadcasted_iota(jnp.int32, tri_ref.shape, 0)
                c_idx = lax.broadcasted_iota(jnp.int32, tri_ref.shape, 1)
                tri_ref[...] = (c_idx <= r_idx).astype(jnp.bfloat16)

        x_in, xn, x_kv, x_q, acc, stash = {}, {}, {}, {}, {}, {}

        def rows(r):
            return slice(r * FFN_SUB, (r + 1) * FFN_SUB)

        def normalize(r, x):
            x_in[r] = x
            xhat = x * lax.rsqrt(jnp.mean(x * x, axis=-1, keepdims=True) + RMS_EPS)
            xn[r] = (xhat * pre_ref[...]).astype(jnp.bfloat16)
            if kv:
                x_kv[r] = (xhat * kvg_ref[...]).astype(jnp.bfloat16)
            acc[r] = jnp.zeros((FFN_SUB, D_MODEL), jnp.float32)

        def attn_out_items(r):
            parts = []

            def make(c):
                def project():
                    parts.append(_dot(og_ref[rows(r), :],
                                      wo_ref[:, c * MXU_COLS:(c + 1) * MXU_COLS]))

                def finish():
                    if c == groups - 1:
                        mixed = jnp.concatenate(parts, axis=1)
                        normalize(r, x_ref[rows(r), :] + _rms(mixed, mixpost_ref[...]))
                return project, None, finish
            return [make(c) for c in range(groups)]

        def ffn_items(r):
            def make(c):
                lo = c * FFN_FC
                key = ("ffn", r, c)

                def project():
                    if c == 0 and not attn_out:
                        normalize(r, x_ref[rows(r), :])
                    stash[key] = (_dot(xn[r], win_ref[:, lo:lo + FFN_FC]),
                                  _dot(xn[r], win_ref[:, D_FF + lo:D_FF + lo + FFN_FC]))

                def activate():
                    gate, up = stash.pop(key)
                    stash[key] = (gate * jax.nn.sigmoid(gate) * up).astype(jnp.bfloat16)

                def output():
                    acc[r] = acc[r] + _dot(stash.pop(key), wout_ref[lo:lo + FFN_FC, :])
                    if c == n_chunks - 1:
                        out = x_in.pop(r) + 0.5 * _rms(acc.pop(r), post_ref[...])
                        o_ref[rows(r), :] = out
                        if qg:
                            x_q[r] = _rms(out, mixpre_ref[...]).astype(jnp.bfloat16)
                return project, activate, output
            return [make(c) for c in range(n_chunks)]

        def kv_items(r):
            ts = FFN_SUB
            lane = lax.broadcasted_iota(jnp.int32, (1, LANES), 1)
            is_head = lane < N_HEADS

            def logit():
                stash["f"] = _dot(x_kv[r], wf_ref[...]) + fb_ref[...]

            def log_forget():
                f_logit = stash.pop("f")
                log_f = jnp.minimum(f_logit, 0.0) - jnp.log1p(jnp.exp(-jnp.abs(f_logit)))
                hi, mid, lo = (part.astype(jnp.float32)
                               for part in _split3(jnp.where(is_head, log_f, 0.0)))
                packed = hi + pltpu.roll(mid, BIAS_MID, axis=1) + pltpu.roll(lo, BIAS_LO, axis=1)
                stash["f"] = packed.astype(jnp.bfloat16)

            def cumulative():
                sums = _dot(tri_ref[...], stash.pop("f"))
                sums = (sums + pltpu.roll(sums, LANES - BIAS_MID, axis=1)
                        + pltpu.roll(sums, LANES - BIAS_LO, axis=1))
                csum = jnp.where(is_head, sums + run_ref[0:1, :], 0.0)
                run_ref[...] = jnp.broadcast_to(csum[ts - 1:ts, :], run_ref.shape)
                hi, mid, lo = (part.astype(jnp.float32) for part in _split3(csum * LOG2_E))
                key_bias = (jnp.where(lane >= BIAS_Q, 1.0, -hi)
                            - pltpu.roll(mid, BIAS_MID, axis=1) - pltpu.roll(lo, BIAS_LO, axis=1))
                query_bias = (pltpu.roll(hi, BIAS_Q, axis=1)
                              + pltpu.roll(mid, BIAS_Q + BIAS_MID, axis=1)
                              + pltpu.roll(lo, BIAS_Q + BIAS_LO, axis=1))
                kb_ref[0] = key_bias.astype(jnp.bfloat16)
                qb_ref[0] = query_bias.astype(jnp.bfloat16)

            def make(c):
                key = ("kv", c)

                def project():
                    stash[key] = (
                        _dot(x_kv[r], wkv_ref[:, c * MXU_COLS:(c + 1) * MXU_COLS]),
                        _dot(x_kv[r],
                             wkv_ref[:, D_MODEL + c * MXU_COLS:D_MODEL + (c + 1) * MXU_COLS]))

                def store():
                    k2, v2 = stash.pop(key)
                    for half in range(halves):
                        p = c * halves + half
                        k_ref[0, p] = k2[:, half * LANES:(half + 1) * LANES].astype(jnp.bfloat16)
                        for blk in range(ts // ATT_T):
                            vt_ref[0, p, blk] = v2[
                                blk * ATT_T:(blk + 1) * ATT_T,
                                half * LANES:(half + 1) * LANES].T.astype(jnp.bfloat16)
                return project, None, store
            return [(logit, log_forget, cumulative)] + [make(c) for c in range(groups)]

        def qg_items(r):
            scale = LOG2_E / math.sqrt(HEAD_DIM)

            def make(c):
                key = ("qg", c)

                def project():
                    stash[key] = _dot(x_q[r], wqg_ref[:, c * MXU_COLS:(c + 1) * MXU_COLS])

                def store():
                    val = stash.pop(key)
                    if c >= groups:
                        gate_ref[:, (c - groups) * MXU_COLS:(c - groups + 1) * MXU_COLS] = val
                        return
                    val = val * scale
                    for half in range(halves):
                        q_ref[0, c * halves + half] = (
                            val[:, half * LANES:(half + 1) * LANES].astype(jnp.bfloat16))
                return project, None, store
            return [make(c) for c in range(2 * groups)]

        bubble = (None, None, None)
        items = []
        if attn_out:
            for r in range(n_sub):
                items += attn_out_items(r)
            if n_sub == 1:
                items += [bubble] * ITEM_LAGS[-1]
        for r in range(n_sub):
            items += ffn_items(r)
        if kv:
            items += kv_items(0)
        if qg:
            items += ([] if kv else [bubble] * ITEM_LAGS[-1]) + qg_items(0)

        for i in range(len(items) + ITEM_LAGS[-1]):
            if i == len(items) // 2:
                _run_side_casts(side_in, side_out)
            for stage, lag in enumerate(ITEM_LAGS):
                if 0 <= i - lag < len(items) and items[i - lag][stage] is not None:
                    items[i - lag][stage]()

    return body


def _ffn(x2d, pre_g, post_g, w_in, w_out, *, tm=FFN_TM, attn_out=None, kv=None, qg=None,
         seq_len=None, side_casts=()):
    t, d = x2d.shape
    row_spec = pl.BlockSpec((tm, d), lambda i: (i, 0))
    vec_spec = _resident((1, d))
    operands, in_specs = [x2d], [row_spec]
    if attn_out is not None:
        og, w_o, mix_post_g = attn_out
        operands += [og, w_o, mix_post_g.reshape(1, -1)]
        in_specs += [row_spec, _resident_weight((d, d)), vec_spec]
    operands += [pre_g.reshape(1, -1), post_g.reshape(1, -1), w_in, w_out]
    in_specs += [vec_spec, vec_spec, _resident_weight((d, 2 * D_FF)),
                 _resident_weight((D_FF, d))]
    out_shape = [jax.ShapeDtypeStruct((t, d), jnp.float32)]
    out_specs = [row_spec]
    scratch = []
    seq_tiles = None
    if kv is not None or qg is not None:
        seq_tiles = seq_len // tm
        bsz = t // seq_len
        pair_shape = jax.ShapeDtypeStruct((bsz, N_PAIRS, seq_len, LANES), jnp.bfloat16)
        pair_spec = pl.BlockSpec((1, N_PAIRS, tm, LANES),
                                 lambda i: (i // seq_tiles, 0, i % seq_tiles, 0))
    if kv is not None:
        kv_g, w_kv, w_f, fb = kv
        operands += [kv_g.reshape(1, -1), w_kv, w_f, fb]
        in_specs += [vec_spec, _resident_weight((d, 2 * d)), _resident((d, LANES)),
                     _resident((1, LANES))]
        bias_shape = jax.ShapeDtypeStruct((bsz, seq_len, LANES), jnp.bfloat16)
        bias_spec = pl.BlockSpec((1, tm, LANES), lambda i: (i // seq_tiles, i % seq_tiles, 0))
        out_shape += [
            pair_shape,
            jax.ShapeDtypeStruct((bsz, N_PAIRS, seq_len // ATT_T, LANES, ATT_T), jnp.bfloat16),
            bias_shape, bias_shape]
        out_specs += [
            pair_spec,
            pl.BlockSpec((1, N_PAIRS, tm // ATT_T, LANES, ATT_T),
                         lambda i: (i // seq_tiles, 0, i % seq_tiles, 0, 0)),
            bias_spec, bias_spec]
        scratch += [pltpu.VMEM((8, LANES), jnp.float32),
                    pltpu.VMEM((tm, tm), jnp.bfloat16)]
    if qg is not None:
        mix_pre_g, w_qg = qg
        operands += [mix_pre_g.reshape(1, -1), w_qg]
        in_specs += [vec_spec, _resident_weight((d, 2 * d))]
        out_shape += [pair_shape, jax.ShapeDtypeStruct((t, d), jnp.float32)]
        out_specs += [pair_spec, row_spec]
    for request in side_casts:
        in_spec, out_spec, shape = _side_cast(request, t // tm, lambda i: i)
        operands.append(request[0])
        in_specs.append(in_spec)
        out_specs.append(out_spec)
        out_shape.append(shape)
    return pl.pallas_call(
        _make_ffn_kernel(tm // FFN_SUB, attn_out is not None, kv is not None, qg is not None,
                         seq_tiles, len(side_casts)),
        name="ffn",
        grid=(t // tm,),
        in_specs=in_specs,
        out_specs=out_specs,
        out_shape=out_shape,
        scratch_shapes=scratch,
        compiler_params=pltpu.CompilerParams(
            dimension_semantics=("arbitrary",), vmem_limit_bytes=VMEM_LIMIT),
    )(*operands)


def _mix_kernel(x_ref, pre_ref, post_ref, win_ref, ck_ref, wout_ref, *rest):
    n_side = (len(rest) - 2) // 2
    o_ref, carry_ref = rest[n_side], rest[-1]

    @pl.when(pl.program_id(1) == 0)
    def _():
        carry_ref[...] = jnp.zeros_like(carry_ref)

    ts = MIX_SUB
    row = lax.broadcasted_iota(jnp.int32, (ts, MIX_CC), 0)
    n_chunks = D_MODEL // MIX_CC
    items = [(r, c) for r in range(x_ref.shape[1] // ts) for c in range(n_chunks)]
    x, xn, acc, proj, gated = {}, {}, {}, {}, {}

    def rows(r):
        return slice(r * ts, (r + 1) * ts)

    def project(i):
        r, c = items[i]
        if c == 0:
            x[r] = x_ref[0, rows(r), :]
            xn[r] = _rms(x[r], pre_ref[...]).astype(jnp.bfloat16)
            acc[r] = jnp.zeros((ts, D_MODEL), jnp.float32)
        lo = c * MIX_CC
        proj[i] = tuple(_dot(xn[r], win_ref[:, part * D_MODEL + lo:part * D_MODEL + lo + MIX_CC])
                        for part in range(3))

    def conv(i):
        lo = items[i][1] * MIX_CC
        b_gate, c_gate, h = proj.pop(i)
        u = c_gate * h
        prev1 = carry_ref[7:8, lo:lo + MIX_CC]
        prev2 = carry_ref[6:7, lo:lo + MIX_CC]
        u1 = jnp.where(row == 0, prev1, pltpu.roll(u, 1, axis=0))
        u2 = jnp.where(row == 0, prev2, jnp.where(row == 1, prev1, pltpu.roll(u, 2, axis=0)))
        carry_ref[:, lo:lo + MIX_CC] = u[ts - 8:, :]
        y = (ck_ref[0:1, lo:lo + MIX_CC] * u2 + ck_ref[1:2, lo:lo + MIX_CC] * u1
             + ck_ref[2:3, lo:lo + MIX_CC] * u)
        gated[i] = (b_gate * y).astype(jnp.bfloat16)

    def output(i):
        r, c = items[i]
        lo = c * MIX_CC
        acc[r] = acc[r] + _dot(gated.pop(i), wout_ref[lo:lo + MIX_CC, :])
        if c == n_chunks - 1:
            o_ref[0, rows(r), :] = x.pop(r) + _rms(acc.pop(r), post_ref[...])

    n = len(items)
    for step in range(n + 2):
        if step < n:
            project(step)
        if step == n // 2:
            _run_side_casts(rest[:n_side], rest[n_side + 1:-1])
        if 1 <= step <= n:
            conv(step - 1)
        if step >= 2:
            output(step - 2)


def _conv_mixer(x, pre_g, post_g, w_in, conv_k, w_out, side_casts=()):
    b, s, d = x.shape
    seq_tiles = s // MIX_TS
    x_spec = pl.BlockSpec((1, MIX_TS, d), lambda i, j: (i, j, 0))
    sides = [_side_cast(request, b * seq_tiles, lambda i, j: i * seq_tiles + j)
             for request in side_casts]
    return pl.pallas_call(
        _mix_kernel,
        name="conv_mixer",
        grid=(b, seq_tiles),
        in_specs=[
            x_spec,
            _resident((1, d)),
            _resident((1, d)),
            _resident_weight((d, 3 * d)),
            _resident((3, d)),
            _resident_weight((d, d)),
        ] + [side[0] for side in sides],
        out_specs=[x_spec] + [side[1] for side in sides],
        out_shape=[jax.ShapeDtypeStruct(x.shape, jnp.float32)] + [side[2] for side in sides],
        scratch_shapes=[pltpu.VMEM((8, d), jnp.float32)],
        compiler_params=pltpu.CompilerParams(
            dimension_semantics=("arbitrary", "arbitrary"), vmem_limit_bytes=VMEM_LIMIT),
    )(x, pre_g.reshape(1, -1), post_g.reshape(1, -1), w_in, conv_k, w_out,
      *[request[0] for request in side_casts])


def _attn_kernel(q_ref, qb_ref, k_ref, kb_ref, vt_ref, gate_ref, o_ref, qa_ref, m_ref, acc_ref):
    t = ATT_T
    n_q = q_ref.shape[2] // t
    lane = lax.broadcasted_iota(jnp.int32, (1, LANES), 1)
    key_idx = lax.broadcasted_iota(jnp.int32, (t, t), 0)
    query_idx = lax.broadcasted_iota(jnp.int32, (t, t), 1)
    causal = key_idx <= query_idx
    ones_rows = jnp.ones((ONES_ROWS, t), jnp.bfloat16)

    def build_query_operands(pp):
        pair = pl.program_id(1) * ATT_PAIRS + pp
        for j in range(2):
            h = 2 * pair + j
            head_mask = (lane >= j * HEAD_DIM) & (lane < (j + 1) * HEAD_DIM)
            bias_mask = ((lane == BIAS_Q + h) | (lane == BIAS_Q + BIAS_MID + h)
                         | (lane == BIAS_Q + BIAS_LO + h))
            bias_ones = jnp.where((lane == h) | (lane == BIAS_MID + h) | (lane == BIAS_LO + h),
                                  1.0, 0.0).astype(jnp.bfloat16)
            for qi in range(n_q):
                q2 = q_ref[0, pp, qi * t:(qi + 1) * t, :]
                qb = qb_ref[0, qi * t:(qi + 1) * t, :]
                qa_ref[pp, qi, j] = jnp.concatenate(
                    [jnp.where(head_mask, q2, jnp.zeros_like(q2)),
                     jnp.where(bias_mask, qb, bias_ones)], axis=1).T

    chains = [(pp, kt, qi, j) for pp in range(ATT_PAIRS) for kt in range(n_q)
              for qi in range(kt, n_q) for j in range(2)]
    per_pair = len(chains) // ATT_PAIRS
    scores, probs, o_t = {}, {}, {}

    def score_stage(i):
        pp, kt, qi, j = chains[i]
        if i % per_pair == per_pair // 2 and pp + 1 < ATT_PAIRS:
            build_query_operands(pp + 1)
        ka = jnp.concatenate([k_ref[0, pp, kt * t:(kt + 1) * t, :],
                              kb_ref[0, kt * t:(kt + 1) * t, :]], axis=1)
        scores[i] = _dot(ka, qa_ref[pp, qi, j])

    def softmax_stage(i):
        pp, kt, qi, j = chains[i]
        s = scores.pop(i)
        if qi == kt:
            s = jnp.where(causal, s, -jnp.inf)
        m_new = jnp.max(s, axis=0, keepdims=True)
        alpha = None
        if kt > 0:
            m_old = m_ref[pp, qi, j]
            m_new = jnp.maximum(m_old, m_new)
            alpha = jnp.exp2(m_old - m_new)
        if qi > kt:
            m_ref[pp, qi, j] = m_new
        probs[i] = (jnp.exp2(s - m_new).astype(jnp.bfloat16), alpha)

    def value_stage(i):
        pp, kt, qi, j = chains[i]
        p, alpha = probs.pop(i)
        v_aug = jnp.concatenate(
            [vt_ref[0, pp, kt, j * HEAD_DIM:(j + 1) * HEAD_DIM, :], ones_rows], axis=0)
        acc = _dot(v_aug, p)
        if kt > 0:
            acc = alpha * acc_ref[pp, qi, j] + acc
        if qi > kt:
            acc_ref[pp, qi, j] = acc
            return
        o_t[j] = acc[:HEAD_DIM] / acc[HEAD_DIM:HEAD_DIM + 1]
        if j == 1:
            o = jnp.concatenate([o_t.pop(0), o_t.pop(1)], axis=0).T
            cols = slice(pp * LANES, (pp + 1) * LANES)
            gate = gate_ref[0, qi * t:(qi + 1) * t, cols]
            o_ref[0, qi * t:(qi + 1) * t, cols] = (jax.nn.sigmoid(gate) * o).astype(jnp.bfloat16)

    build_query_operands(0)
    n = len(chains)
    for i in range(n + VALUE_LAG):
        if i < n:
            score_stage(i)
        if SOFTMAX_LAG <= i < n + SOFTMAX_LAG:
            softmax_stage(i - SOFTMAX_LAG)
        if i >= VALUE_LAG:
            value_stage(i - VALUE_LAG)


def _attention(q, q_bias, k, k_bias, v_t, gate):
    b, _, s, _ = q.shape
    n_q = s // ATT_T
    qk_spec = pl.BlockSpec((1, ATT_PAIRS, s, LANES), lambda i, g: (i, g, 0, 0))
    bias_spec = pl.BlockSpec((1, s, LANES), lambda i, g: (i, 0, 0))
    slab_spec = pl.BlockSpec((1, s, ATT_PAIRS * LANES), lambda i, g: (i, 0, g))
    return pl.pallas_call(
        _attn_kernel,
        name="fox_attention",
        grid=(b, N_PAIRS // ATT_PAIRS),
        in_specs=[
            qk_spec, bias_spec, qk_spec, bias_spec,
            pl.BlockSpec((1, ATT_PAIRS, s // ATT_T, LANES, ATT_T), lambda i, g: (i, g, 0, 0, 0)),
            slab_spec,
        ],
        out_specs=slab_spec,
        out_shape=jax.ShapeDtypeStruct((b, s, D_MODEL), jnp.bfloat16),
        scratch_shapes=[
            pltpu.VMEM((ATT_PAIRS, n_q, 2, 2 * LANES, ATT_T), jnp.bfloat16),
            pltpu.VMEM((ATT_PAIRS, n_q, 2, 1, ATT_T), jnp.float32),
            pltpu.VMEM((ATT_PAIRS, n_q, 2, HEAD_DIM + ONES_ROWS, ATT_T), jnp.float32),
        ],
        compiler_params=pltpu.CompilerParams(
            dimension_semantics=("arbitrary", "arbitrary"), vmem_limit_bytes=VMEM_LIMIT),
    )(q, q_bias, k, k_bias, v_t, gate)


def kernel(x, ffn1_pre_g, ffn1_post_g, ffn1_w_in, ffn1_w_out, mix_pre_g, mix_post_g,
           ffn2_pre_g, ffn2_post_g, ffn2_w_in, ffn2_w_out, conv_w_in, conv_k, conv_w_out,
           kv_g, kv_w, forget_b, attn_w_qg, attn_w_o):
    b, s, d = x.shape

    h, conv_in, conv_out, w2_in0, w2_out0 = _ffn(
        x.reshape(b * s, d), ffn1_pre_g[0], ffn1_post_g[0],
        _to_bf16(ffn1_w_in, 0), _to_bf16(ffn1_w_out, 0),
        side_casts=[(conv_w_in, 0, None), (conv_w_out, 0, None),
                    (ffn2_w_in, 0, None), (ffn2_w_out, 0, None)])
    h, w1_in1, w1_out1, w_kv, w_qg = _conv_mixer(
        h.reshape(b, s, d), mix_pre_g[0], mix_post_g[0], conv_in, conv_k[0], conv_out,
        side_casts=[(ffn1_w_in, 1, None), (ffn1_w_out, 1, None), (kv_w.T, 0, 2 * d),
                    (attn_w_qg, 0, None)])
    h, w2_in1, w2_out1, w_o = _ffn(
        h.reshape(b * s, d), ffn2_pre_g[0], ffn2_post_g[0], w2_in0, w2_out0,
        side_casts=[(ffn2_w_in, 1, None), (ffn2_w_out, 1, None), (attn_w_o, 0, None)])

    w_f = jnp.pad(kv_w[:, 2 * d:], ((0, 0), (0, LANES - N_HEADS))).astype(jnp.bfloat16)
    fb = jnp.pad(forget_b, (0, LANES - N_HEADS)).reshape(1, LANES)
    h, k, v_t, k_bias, q_bias, q, gate = _ffn(
        h, ffn1_pre_g[1], ffn1_post_g[1], w1_in1, w1_out1, tm=FFN_SUB, seq_len=s,
        kv=(kv_g, w_kv, w_f, fb), qg=(mix_pre_g[1], w_qg))

    o = _attention(q, q_bias, k, k_bias, v_t, gate.reshape(b, s, d))
    (h,) = _ffn(h, ffn2_pre_g[1], ffn2_post_g[1], w2_in1, w2_out1,
                attn_out=(o.reshape(b * s, d), w_o, mix_post_g[1]))
    return h.reshape(b, s, d)
```
